```python
import jax, jax.numpy as jnp
from jax import lax
import numpy as np

D_MODEL = 1024
BATCH = 32
SEQ = 2048
DEPTH = 1

CHUNK = 64
Q_BLOCK = 128
ATT_HEADS = 16
HEAD_DIM = 64
D_ATT = ATT_HEADS * HEAD_DIM
D_RNN = D_MODEL
RNN_BLOCKS = 16
RNN_BLOCK_W = D_RNN // RNN_BLOCKS
CONV_W = 4
RG_C = 8.0
NORM_EPS = 1e-6
MASK_VALUE = -1e30
IN_WIDTHS = (D_ATT, D_ATT, D_ATT, ATT_HEADS, D_ATT, D_RNN, D_RNN, D_MODEL, D_MODEL)
IN_TOTAL = 5 * D_ATT + ATT_HEADS + 2 * D_RNN + 2 * D_MODEL

kernel_name = "hybrid_fox_rglru_gated_block"


def rms_norm(x, w):
    xf = x.astype(jnp.float32)
    y = xf * lax.rsqrt(jnp.mean(xf * xf, axis=-1, keepdims=True) + NORM_EPS)
    return (y * w.astype(jnp.float32)).astype(x.dtype)


def split_columns(z):
    parts = []
    start = 0
    for width in IN_WIDTHS:
        parts.append(z[..., start:start + width])
        start += width
    return parts


def forgetting_attention(q, k, v, log_f):
    seq = q.shape[1]
    scale = HEAD_DIM ** -0.5
    c = jnp.transpose(jnp.cumsum(log_f, axis=1), (0, 2, 1))
    outs = []
    for blk in range(seq // Q_BLOCK):
        q0 = blk * Q_BLOCK
        q1 = q0 + Q_BLOCK
        qb = q[:, q0:q1]
        kb = k[:, :q1]
        vb = v[:, :q1]
        s = jnp.einsum('bqhd,bkhd->bhqk', qb, kb).astype(jnp.float32) * scale
        s = s + c[:, :, q0:q1][:, :, :, None] - c[:, :, :q1][:, :, None, :]
        mask = (q0 + jnp.arange(Q_BLOCK))[:, None] >= jnp.arange(q1)[None, :]
        s = jnp.where(mask[None, None], s, MASK_VALUE)
        p = jax.nn.softmax(s, axis=-1)
        outs.append(jnp.einsum('bhqk,bkhd->bqhd', p.astype(vb.dtype), vb))
    return jnp.concatenate(outs, axis=1)


def causal_depthwise_conv(x, w, b):
    seq = x.shape[1]
    xp = jnp.pad(x, ((0, 0), (CONV_W - 1, 0), (0, 0)))
    y = b + w[0] * xp[:, 0:seq]
    for j in range(1, CONV_W):
        y = y + w[j] * xp[:, j:j + seq]
    return y


def rg_lru(x, r, i, lam):
    log_a = -RG_C * r.astype(jnp.float32) * jax.nn.softplus(-lam.astype(jnp.float32))
    a = jnp.exp(log_a)
    u = jnp.sqrt(jnp.maximum(-jnp.expm1(2.0 * log_a), 0.0)) * (
        i.astype(jnp.float32) * x.astype(jnp.float32))
    a_t = jnp.swapaxes(a, 0, 1)
    u_t = jnp.swapaxes(u, 0, 1)

    def step(h, inp):
        a_s, u_s = inp
        h = a_s * h + u_s
        return h, h

    h0 = jnp.zeros(a_t.shape[1:], jnp.float32)
    _, hs = lax.scan(step, h0, (a_t, u_t))
    return jnp.swapaxes(hs, 0, 1).astype(x.dtype)


def setup_inputs(seed: int = 0) -> dict:
    key = jax.random.key(seed)
    ks = jax.random.split(key, 18)
    f32 = jnp.float32
    L = DEPTH

    def nrm(k, shape, fan_in):
        return jax.random.normal(k, shape, f32) * (fan_in ** -0.5)

    x = jax.random.normal(ks[0], (BATCH, SEQ, D_MODEL), f32)
    pre_norm_w = 1.0 + 0.05 * jax.random.normal(ks[1], (L, D_MODEL), f32)
    w_in = nrm(ks[2], (L, D_MODEL, IN_TOTAL), D_MODEL)
    b_in = 0.02 * jax.random.normal(ks[3], (L, IN_TOTAL), f32)
    conv_w = nrm(ks[4], (L, CONV_W, D_RNN), CONV_W)
    conv_b = 0.02 * jax.random.normal(ks[5], (L, D_RNN), f32)
    rg_wa = nrm(ks[6], (L, RNN_BLOCKS, RNN_BLOCK_W, RNN_BLOCK_W), RNN_BLOCK_W)
    rg_ba = 0.02 * jax.random.normal(ks[7], (L, D_RNN), f32)
    rg_wx = nrm(ks[8], (L, RNN_BLOCKS, RNN_BLOCK_W, RNN_BLOCK_W), RNN_BLOCK_W)
    rg_bx = 0.02 * jax.random.normal(ks[9], (L, D_RNN), f32)
    u = jax.random.uniform(ks[10], (L, D_RNN), f32, minval=0.9, maxval=0.999)
    a0 = u ** (1.0 / RG_C)
    rg_lambda = jnp.log(a0) - jnp.log1p(-a0)
    w_branch_a = nrm(ks[11], (L, D_ATT, D_MODEL), D_ATT)
    w_branch_r = nrm(ks[12], (L, D_RNN, D_MODEL), D_RNN)
    w_out = nrm(ks[13], (L, D_MODEL, D_MODEL), D_MODEL)
    post_norm_w = 1.0 + 0.05 * jax.random.normal(ks[14], (L, D_MODEL), f32)
    return {"x": x, "pre_norm_w": pre_norm_w, "w_in": w_in, "b_in": b_in,
            "conv_w": conv_w, "conv_b": conv_b, "rg_wa": rg_wa, "rg_ba": rg_ba,
            "rg_wx": rg_wx, "rg_bx": rg_bx, "rg_lambda": rg_lambda,
            "w_branch_a": w_branch_a, "w_branch_r": w_branch_r, "w_out": w_out,
            "post_norm_w": post_norm_w}


def reference(x, pre_norm_w, w_in, b_in, conv_w, conv_b, rg_wa, rg_ba, rg_wx, rg_bx,
              rg_lambda, w_branch_a, w_branch_r, w_out, post_norm_w):
    bsz, seq, _ = x.shape
    for l in range(DEPTH):
        h = rms_norm(x, pre_norm_w[l])
        z = jnp.einsum('bsd,de->bse', h, w_in[l]) + b_in[l]
        q, k, v, f_logit, gate_a, x_r, gate_r, mg_a, mg_r = split_columns(z)

        q = q.reshape(bsz, seq, ATT_HEADS, HEAD_DIM)
        k = k.reshape(bsz, seq, ATT_HEADS, HEAD_DIM)
        v = v.reshape(bsz, seq, ATT_HEADS, HEAD_DIM)
        log_f = jax.nn.log_sigmoid(f_logit.astype(jnp.float32))
        y_a = forgetting_attention(q, k, v, log_f).reshape(bsz, seq, D_ATT)
        y_a = jnp.einsum('bsc,cd->bsd', y_a * jax.nn.silu(gate_a), w_branch_a[l])

        xc = causal_depthwise_conv(x_r, conv_w[l], conv_b[l])
        xb = xc.reshape(bsz, seq, RNN_BLOCKS, RNN_BLOCK_W)
        r = jax.nn.sigmoid(jnp.einsum('bsgi,gij->bsgj', xb, rg_wa[l]).reshape(bsz, seq, D_RNN) + rg_ba[l])
        i = jax.nn.sigmoid(jnp.einsum('bsgi,gij->bsgj', xb, rg_wx[l]).reshape(bsz, seq, D_RNN) + rg_bx[l])
        y_r = rg_lru(xc, r, i, rg_lambda[l])
        y_r = jnp.einsum('bsc,cd->bsd', y_r * jax.nn.silu(gate_r), w_branch_r[l])

        m = jax.nn.sigmoid(mg_a) * y_a + jax.nn.sigmoid(mg_r) * y_r
        o = jnp.einsum('bsd,de->bse', m, w_out[l])
        x = x + rms_norm(o, post_norm_w[l])
    return x
```

```python
import functools

import jax
import jax.numpy as jnp
from jax import lax
from jax.experimental import pallas as pl
from jax.experimental.pallas import tpu as pltpu

D_MODEL = 1024
ATT_HEADS = 16
HEAD_DIM = 64
RNN_BLOCKS = 16
RNN_BLOCK_W = 64
CONV_W = 4
RG_C = 8.0
NORM_EPS = 1e-6
MASK_VALUE = -1e30

LANES = 128
MXU_DIM = 256
N_GROUPS = 8
AUG_PER_HEAD = 6
VMEM_LIMIT = 56 * 1024 * 1024

BM_IN = 512
BM_OUT = 512
T_RNN = 256
TQ = 256

F32 = jnp.float32
BF16 = jnp.bfloat16


def _sigmoid(x):
    return 1.0 / (1.0 + jnp.exp(-x))


def _silu(x):
    return x * _sigmoid(x)


def _log_sigmoid(x):
    return jnp.minimum(x, 0.0) - jnp.log(1.0 + jnp.exp(-jnp.abs(x)))


def _softplus(x):
    return jnp.maximum(x, 0.0) + jnp.log(1.0 + jnp.exp(-jnp.abs(x)))


def _split3(x):
    hi = x.astype(BF16)
    r1 = x - hi.astype(F32)
    mid = r1.astype(BF16)
    lo = (r1 - mid.astype(F32)).astype(BF16)
    return hi, mid, lo


def _in_proj_kernel(x_ref, nw_ref, w_ref, b_ref, wf_ref, bf_ref, z_ref, lf_ref):
    x = x_ref[...]
    var = jnp.mean(x * x, axis=-1, keepdims=True)
    h = (x * lax.rsqrt(var + NORM_EPS) * nw_ref[...]).astype(BF16)
    for g in range(N_GROUPS):
        cols = slice(g * D_MODEL, (g + 1) * D_MODEL)
        acc = jnp.dot(h, w_ref[:, cols], preferred_element_type=F32) + b_ref[:, cols]
        if g == 0:
            val = acc * (HEAD_DIM ** -0.5)
        elif g in (3, 5):
            val = _silu(acc)
        elif g in (6, 7):
            val = _sigmoid(acc)
        else:
            val = acc
        z_ref[:, cols] = val.astype(BF16)
    f = jnp.dot(h, wf_ref[...], preferred_element_type=F32) + bf_ref[...]
    lf_ref[...] = _log_sigmoid(f)


def _in_proj(x2, nw, w8, b8, wf, bf):
    m = x2.shape[0]
    const = lambda i: (0, 0)
    return pl.pallas_call(
        _in_proj_kernel,
        grid=(m // BM_IN,),
        in_specs=[
            pl.BlockSpec((BM_IN, D_MODEL), lambda i: (i, 0)),
            pl.BlockSpec((1, D_MODEL), const),
            pl.BlockSpec((D_MODEL, N_GROUPS * D_MODEL), const, pipeline_mode=pl.Buffered(1)),
            pl.BlockSpec((1, N_GROUPS * D_MODEL), const),
            pl.BlockSpec((D_MODEL, LANES), const),
            pl.BlockSpec((1, LANES), const),
        ],
        out_specs=[
            pl.BlockSpec((BM_IN, N_GROUPS * D_MODEL), lambda i: (i, 0)),
            pl.BlockSpec((BM_IN, LANES), lambda i: (i, 0)),
        ],
        out_shape=[
            jax.ShapeDtypeStruct((m, N_GROUPS * D_MODEL), BF16),
            jax.ShapeDtypeStruct((m, LANES), F32),
        ],
        compiler_params=pltpu.CompilerParams(
            dimension_semantics=("arbitrary",), vmem_limit_bytes=VMEM_LIMIT),
        name="in_proj",
    )(x2, nw, w8, b8, wf, bf)


def _attn_kernel(q_ref, k_ref, v_ref, ga_ref, lf_ref, o_ref, qaug_ref, kaug_ref, *, seq):
    j = pl.program_id(1)
    n_q = seq // TQ

    @pl.when(j == 0)
    def _build_aug():
        row = lax.broadcasted_iota(jnp.int32, (LANES, LANES), 0)
        col = lax.broadcasted_iota(jnp.int32, (LANES, LANES), 1)
        tri = (col <= row).astype(BF16)
        lane1 = lax.broadcasted_iota(jnp.int32, (1, LANES), 1)
        n_aug = ATT_HEADS * AUG_PER_HEAD
        k_ones = ((lane1 % AUG_PER_HEAD >= 3) & (lane1 < n_aug)).astype(F32)
        q_ones = ((lane1 % AUG_PER_HEAD < 3) & (lane1 < n_aug)).astype(F32)
        sel = [((col == AUG_PER_HEAD * row + r) & (row < ATT_HEADS)).astype(BF16)
               for r in range(AUG_PER_HEAD)]
        carry = jnp.zeros((1, LANES), F32)
        for blk in range(seq // LANES):
            rows = slice(blk * LANES, (blk + 1) * LANES)
            pieces = _split3(lf_ref[rows, :])
            c = carry
            for p in pieces:
                c = c + jnp.dot(tri, p, preferred_element_type=F32)
            carry = c[LANES - 1:LANES, :]
            c3 = _split3(c)
            kacc = k_ones
            qacc = q_ones
            for r in range(3):
                kacc = kacc - jnp.dot(c3[r], sel[r], preferred_element_type=F32)
                qacc = qacc + jnp.dot(c3[r], sel[3 + r], preferred_element_type=F32)
            kaug_ref[rows, :] = kacc.astype(BF16)
            qaug_ref[rows, :] = qacc.astype(BF16)

    lane_q = lax.broadcasted_iota(jnp.int32, (TQ, LANES), 1)
    r_iota = lax.broadcasted_iota(jnp.int32, (TQ, TQ), 0)
    c_iota = lax.broadcasted_iota(jnp.int32, (TQ, TQ), 1)
    nt_dims = (((1,), (1,)), ((), ()))

    def q_body(i, _):
        q0 = pl.multiple_of(i * TQ, TQ)
        q_t = q_ref[pl.ds(q0, TQ), :]
        qa_t = qaug_ref[pl.ds(q0, TQ), :]
        ys = []
        for e in range(2):
            head = 2 * j + e
            qm = jnp.where(lane_q // HEAD_DIM == e, q_t, jnp.zeros_like(q_t))
            in_head = (lane_q >= AUG_PER_HEAD * head) & (lane_q < AUG_PER_HEAD * (head + 1))
            qa = jnp.where(in_head, qa_t, jnp.zeros_like(qa_t))
            qcat = jnp.concatenate([qm, qa], axis=1)

            def tile(k0, carry, masked):
                m_i, l_i, acc = carry
                kcat = jnp.concatenate(
                    [k_ref[pl.ds(k0, TQ), :], kaug_ref[pl.ds(k0, TQ), :]], axis=1)
                s = lax.dot_general(qcat, kcat, nt_dims, preferred_element_type=F32)
                if masked:
                    s = jnp.where(r_iota >= c_iota, s, MASK_VALUE)
                m_new = jnp.maximum(m_i, jnp.max(s, axis=1, keepdims=True))
                alpha = jnp.exp(m_i - m_new)
                p = jnp.exp(s - m_new)
                l_new = alpha * l_i + jnp.sum(p, axis=1, keepdims=True)
                pv = jnp.dot(p.astype(BF16), v_ref[pl.ds(k0, TQ), :],
                             preferred_element_type=F32)
                return m_new, l_new, alpha * acc + pv

            init = (jnp.full((TQ, 1), MASK_VALUE, F32), jnp.zeros((TQ, 1), F32),
                    jnp.zeros((TQ, LANES), F32))
            carry = lax.fori_loop(
                0, i, lambda kk, c: tile(pl.multiple_of(kk * TQ, TQ), c, False), init)
            m_i, l_i, acc = tile(q0, carry, True)
            ys.append(acc * (1.0 / l_i))
        y = jnp.where(lane_q < HEAD_DIM, ys[0], ys[1])
        o_ref[pl.ds(q0, TQ), :] = (y * ga_ref[pl.ds(q0, TQ), :].astype(F32)).astype(BF16)
        return 0

    lax.fori_loop(0, n_q, q_body, 0)


def _attention(z, lf, bsz, seq):
    m = bsz * seq
    n_pairs = ATT_HEADS // 2
    col = lambda g: (lambda b, j: (b, g * n_pairs + j))
    blk = (seq, LANES)
    return pl.pallas_call(
        functools.partial(_attn_kernel, seq=seq),
        grid=(bsz, n_pairs),
        in_specs=[
            pl.BlockSpec(blk, col(0)),
            pl.BlockSpec(blk, col(1)),
            pl.BlockSpec(blk, col(2)),
            pl.BlockSpec(blk, col(3)),
            pl.BlockSpec(blk, lambda b, j: (b, 0)),
        ],
        out_specs=pl.BlockSpec(blk, lambda b, j: (b, j)),
        out_shape=jax.ShapeDtypeStruct((m, D_MODEL), BF16),
        scratch_shapes=[pltpu.VMEM(blk, BF16), pltpu.VMEM(blk, BF16)],
        compiler_params=pltpu.CompilerParams(
            dimension_semantics=("arbitrary", "arbitrary"), vmem_limit_bytes=VMEM_LIMIT),
        name="fox_attention",
    )(z, z, z, z, lf)


def _rglru_kernel(xr_ref, gr_ref, cw_ref, cb_ref, wa_ref, ba_ref, wx_ref, bx_ref, lam_ref,
                  o_ref, xs_ref, a_ref, u_ref, hs_ref, h_ref):
    t = pl.program_id(1)
    halo = 8

    @pl.when(t == 0)
    def _reset():
        xs_ref[0:halo, :] = jnp.zeros((halo, D_MODEL), F32)
        h_ref[...] = jnp.zeros_like(h_ref)

    xs_ref[halo:halo + T_RNN, :] = xr_ref[...].astype(F32)
    xc = cb_ref[...] + cw_ref[0:1, :] * xs_ref[pl.ds(halo - 3, T_RNN), :]
    for tap in range(1, CONV_W):
        xc = xc + cw_ref[tap:tap + 1, :] * xs_ref[pl.ds(halo - 3 + tap, T_RNN), :]
    xs_ref[0:halo, :] = xs_ref[T_RNN:T_RNN + halo, :]

    xcb = xc.astype(BF16)
    sp = _softplus(-lam_ref[...])
    for kb in range(D_MODEL // MXU_DIM):
        cols = slice(kb * MXU_DIM, (kb + 1) * MXU_DIM)
        xk = xcb[:, cols]
        r = _sigmoid(jnp.dot(xk, wa_ref[kb], preferred_element_type=F32) + ba_ref[:, cols])
        ig = _sigmoid(jnp.dot(xk, wx_ref[kb], preferred_element_type=F32) + bx_ref[:, cols])
        log_a = -RG_C * r * sp[:, cols]
        a = jnp.exp(log_a)
        one_m_a2 = (1.0 + a * a) * jnp.tanh(-log_a)
        a_ref[:, cols] = a
        u_ref[:, cols] = jnp.sqrt(jnp.maximum(one_m_a2, 0.0)) * (ig * xc[:, cols])

    def step(s, h):
        h = a_ref[pl.ds(s, 1), :] * h + u_ref[pl.ds(s, 1), :]
        hs_ref[pl.ds(s, 1), :] = h
        return h

    h_ref[...] = lax.fori_loop(0, T_RNN, step, h_ref[...], unroll=8)
    o_ref[...] = (hs_ref[...] * gr_ref[...].astype(F32)).astype(BF16)


def _rglru(z, cw, cb, wa_bd, ba, wx_bd, bx, lam, bsz, seq):
    m = bsz * seq
    n_t = seq // T_RNN
    const2 = lambda b, t: (0, 0)
    const3 = lambda b, t: (0, 0, 0)
    n_kb = D_MODEL // MXU_DIM
    return pl.pallas_call(
        _rglru_kernel,
        grid=(bsz, n_t),
        in_specs=[
            pl.BlockSpec((T_RNN, D_MODEL), lambda b, t: (b * n_t + t, 4)),
            pl.BlockSpec((T_RNN, D_MODEL), lambda b, t: (b * n_t + t, 5)),
            pl.BlockSpec((CONV_W, D_MODEL), const2),
            pl.BlockSpec((1, D_MODEL), const2),
            pl.BlockSpec((n_kb, MXU_DIM, MXU_DIM), const3),
            pl.BlockSpec((1, D_MODEL), const2),
            pl.BlockSpec((n_kb, MXU_DIM, MXU_DIM), const3),
            pl.BlockSpec((1, D_MODEL), const2),
            pl.BlockSpec((1, D_MODEL), const2),
        ],
        out_specs=pl.BlockSpec((T_RNN, D_MODEL), lambda b, t: (b * n_t + t, 0)),
        out_shape=jax.ShapeDtypeStruct((m, D_MODEL), BF16),
        scratch_shapes=[
            pltpu.VMEM((T_RNN + 8, D_MODEL), F32),
            pltpu.VMEM((T_RNN, D_MODEL), F32),
            pltpu.VMEM((T_RNN, D_MODEL), F32),
            pltpu.VMEM((T_RNN, D_MODEL), F32),
            pltpu.VMEM((1, D_MODEL), F32),
        ],
        compiler_params=pltpu.CompilerParams(
            dimension_semantics=("arbitrary", "arbitrary"), vmem_limit_bytes=VMEM_LIMIT),
        name="rglru",
    )(z, z, cw, cb, wa_bd, ba, wx_bd, bx, lam)


def _out_kernel(x_ref, ya_ref, yr_ref, sa_ref, sr_ref, wa_ref, wr_ref, wo_ref, pw_ref, o_ref):
    pa = jnp.dot(ya_ref[...], wa_ref[...], preferred_element_type=F32)
    pr = jnp.dot(yr_ref[...], wr_ref[...], preferred_element_type=F32)
    mrg = sa_ref[...].astype(F32) * pa + sr_ref[...].astype(F32) * pr
    o = jnp.dot(mrg.astype(BF16), wo_ref[...], preferred_element_type=F32)
    var = jnp.mean(o * o, axis=-1, keepdims=True)
    o_ref[...] = x_ref[...] + o * lax.rsqrt(var + NORM_EPS) * pw_ref[...]


def _out_stage(x2, ya, yr, z, wa, wr, wo, pw):
    m = x2.shape[0]
    const = lambda i: (0, 0)
    row = lambda i: (i, 0)
    blk = (BM_OUT, D_MODEL)
    wblk = (D_MODEL, D_MODEL)
    return pl.pallas_call(
        _out_kernel,
        grid=(m // BM_OUT,),
        in_specs=[
            pl.BlockSpec(blk, row),
            pl.BlockSpec(blk, row),
            pl.BlockSpec(blk, row),
            pl.BlockSpec(blk, lambda i: (i, 6)),
            pl.BlockSpec(blk, lambda i: (i, 7)),
            pl.BlockSpec(wblk, const),
            pl.BlockSpec(wblk, const),
            pl.BlockSpec(wblk, const),
            pl.BlockSpec((1, D_MODEL), const),
        ],
        out_specs=pl.BlockSpec(blk, row),
        out_shape=jax.ShapeDtypeStruct((m, D_MODEL), F32),
        compiler_params=pltpu.CompilerParams(
            dimension_semantics=("arbitrary",), vmem_limit_bytes=VMEM_LIMIT),
        name="out_stage",
    )(x2, ya, yr, z, z, wa, wr, wo, pw)


def _block_diag(w):
    per = MXU_DIM // RNN_BLOCK_W
    n_kb = RNN_BLOCKS // per
    w4 = w.reshape(n_kb, per, RNN_BLOCK_W, RNN_BLOCK_W)
    eye = jnp.eye(per, dtype=w.dtype)
    bd = jnp.einsum('kgij,gh->kgihj', w4, eye)
    return bd.reshape(n_kb, MXU_DIM, MXU_DIM).astype(BF16)


def kernel(x, pre_norm_w, w_in, b_in, conv_w, conv_b, rg_wa, rg_ba, rg_wx, rg_bx, rg_lambda,
           w_branch_a, w_branch_r, w_out, post_norm_w):
    bsz, seq, d = x.shape
    assert d == D_MODEL and seq % TQ == 0 and seq % T_RNN == 0
    assert (bsz * seq) % BM_IN == 0 and (bsz * seq) % BM_OUT == 0
    d_att = ATT_HEADS * HEAD_DIM
    f0 = 3 * d_att
    f1 = f0 + ATT_HEADS
    x2 = x.reshape(bsz * seq, d)
    row = lambda v: v.reshape(1, -1)
    for l in range(pre_norm_w.shape[0]):
        w = w_in[l]
        w8 = jnp.concatenate([w[:, :f0], w[:, f1:]], axis=1).astype(BF16)
        b8 = row(jnp.concatenate([b_in[l, :f0], b_in[l, f1:]]))
        wf = jnp.pad(w[:, f0:f1], ((0, 0), (0, LANES - ATT_HEADS))).astype(BF16)
        bf = row(jnp.pad(b_in[l, f0:f1], (0, LANES - ATT_HEADS)))
        z, lf = _in_proj(x2, row(pre_norm_w[l]), w8, b8, wf, bf)
        ya = _attention(z, lf, bsz, seq)
        yr = _rglru(z, conv_w[l], row(conv_b[l]), _block_diag(rg_wa[l]), row(rg_ba[l]),
                    _block_diag(rg_wx[l]), row(rg_bx[l]), row(rg_lambda[l]), bsz, seq)
        x2 = _out_stage(x2, ya, yr, z, w_branch_a[l].astype(BF16), w_branch_r[l].astype(BF16),
                        w_out[l].astype(BF16), row(post_norm_w[l]))
    return x2.reshape(bsz, seq, d)
```

```python
import functools

import jax
import jax.numpy as jnp
from jax import lax
from jax.experimental import pallas as pl
from jax.experimental.pallas import tpu as pltpu

D_MODEL = 1024
ATT_HEADS = 16
HEAD_DIM = 64
RNN_BLOCKS = 16
RNN_BLOCK_W = 64
CONV_W = 4
RG_C = 8.0
NORM_EPS = 1e-6
MASK_VALUE = -1e30

LANES = 128
MXU_DIM = 256
N_GROUPS = 8
AUG_PER_HEAD = 6
VMEM_LIMIT = 56 * 1024 * 1024

BM_IN = 512
BM_OUT = 512
T_RNN = 256
TQ = 256
VT_PAD = 16
VT_ROWS = HEAD_DIM + VT_PAD
LOG2E = 1.4426950408889634

F32 = jnp.float32
BF16 = jnp.bfloat16


def _sigmoid(x):
    return 1.0 / (1.0 + jnp.exp(-x))


def _silu(x):
    return x * _sigmoid(x)


def _log_sigmoid(x):
    return jnp.minimum(x, 0.0) - jnp.log(1.0 + jnp.exp(-jnp.abs(x)))


def _softplus(x):
    return jnp.maximum(x, 0.0) + jnp.log(1.0 + jnp.exp(-jnp.abs(x)))


def _split3(x):
    hi = x.astype(BF16)
    r1 = x - hi.astype(F32)
    mid = r1.astype(BF16)
    lo = (r1 - mid.astype(F32)).astype(BF16)
    return hi, mid, lo


def _in_proj_kernel(x_ref, nw_ref, w_ref, b_ref, wf_ref, bf_ref, z_ref, lf_ref):
    x = x_ref[...]
    var = jnp.mean(x * x, axis=-1, keepdims=True)
    h = (x * lax.rsqrt(var + NORM_EPS) * nw_ref[...]).astype(BF16)
    for g in range(N_GROUPS):
        cols = slice(g * D_MODEL, (g + 1) * D_MODEL)
        acc = jnp.dot(h, w_ref[:, cols], preferred_element_type=F32) + b_ref[:, cols]
        if g == 0:
            val = acc * (HEAD_DIM ** -0.5 * LOG2E)
        elif g in (3, 5):
            val = _silu(acc)
        elif g in (6, 7):
            val = _sigmoid(acc)
        else:
            val = acc
        z_ref[:, cols] = val.astype(BF16)
    f = jnp.dot(h, wf_ref[...], preferred_element_type=F32) + bf_ref[...]
    lf_ref[...] = _log_sigmoid(f)


def _in_proj(x2, nw, w8, b8, wf, bf):
    m = x2.shape[0]
    const = lambda i: (0, 0)
    return pl.pallas_call(
        _in_proj_kernel,
        grid=(m // BM_IN,),
        in_specs=[
            pl.BlockSpec((BM_IN, D_MODEL), lambda i: (i, 0)),
            pl.BlockSpec((1, D_MODEL), const),
            pl.BlockSpec((D_MODEL, N_GROUPS * D_MODEL), const, pipeline_mode=pl.Buffered(1)),
            pl.BlockSpec((1, N_GROUPS * D_MODEL), const),
            pl.BlockSpec((D_MODEL, LANES), const),
            pl.BlockSpec((1, LANES), const),
        ],
        out_specs=[
            pl.BlockSpec((BM_IN, N_GROUPS * D_MODEL), lambda i: (i, 0)),
            pl.BlockSpec((BM_IN, LANES), lambda i: (i, 0)),
        ],
        out_shape=[
            jax.ShapeDtypeStruct((m, N_GROUPS * D_MODEL), BF16),
            jax.ShapeDtypeStruct((m, LANES), F32),
        ],
        compiler_params=pltpu.CompilerParams(
            dimension_semantics=("arbitrary",), vmem_limit_bytes=VMEM_LIMIT),
        name="in_proj",
    )(x2, nw, w8, b8, wf, bf)


def _attn_kernel(q_ref, k_ref, v_ref, ga_ref, lf_ref, o_ref, qaug_ref, kaug_ref, vt_ref, *, seq):
    j = pl.program_id(1)
    n_q = seq // TQ

    @pl.when(j == 0)
    def _build_aug():
        row = lax.broadcasted_iota(jnp.int32, (LANES, LANES), 0)
        col = lax.broadcasted_iota(jnp.int32, (LANES, LANES), 1)
        tri = (col <= row).astype(BF16)
        lane1 = lax.broadcasted_iota(jnp.int32, (1, LANES), 1)
        n_aug = ATT_HEADS * AUG_PER_HEAD
        k_ones = ((lane1 % AUG_PER_HEAD >= 3) & (lane1 < n_aug)).astype(F32)
        q_ones = ((lane1 % AUG_PER_HEAD < 3) & (lane1 < n_aug)).astype(F32)
        sel = [((col == AUG_PER_HEAD * row + r) & (row < ATT_HEADS)).astype(BF16)
               for r in range(AUG_PER_HEAD)]
        carry = jnp.zeros((1, LANES), F32)
        for blk in range(seq // LANES):
            rows = slice(blk * LANES, (blk + 1) * LANES)
            pieces = _split3(lf_ref[rows, :] * LOG2E)
            c = carry
            for p in pieces:
                c = c + jnp.dot(tri, p, preferred_element_type=F32)
            carry = c[LANES - 1:LANES, :]
            c3 = _split3(c)
            kacc = k_ones
            qacc = q_ones
            for r in range(3):
                kacc = kacc - jnp.dot(c3[r], sel[r], preferred_element_type=F32)
                qacc = qacc + jnp.dot(c3[r], sel[3 + r], preferred_element_type=F32)
            kaug_ref[rows, :] = kacc.astype(BF16)
            qaug_ref[rows, :] = qacc.astype(BF16)

    v_t = v_ref[...].astype(F32).T.astype(BF16)
    one_row = (lax.broadcasted_iota(jnp.int32, (VT_PAD, seq), 0) == 0).astype(BF16)
    for e in range(2):
        vt_ref[e * VT_ROWS:e * VT_ROWS + HEAD_DIM, :] = v_t[e * HEAD_DIM:(e + 1) * HEAD_DIM, :]
        vt_ref[e * VT_ROWS + HEAD_DIM:(e + 1) * VT_ROWS, :] = one_row
    lane_q = lax.broadcasted_iota(jnp.int32, (TQ, LANES), 1)
    key_i = lax.broadcasted_iota(jnp.int32, (TQ, TQ), 0)
    qry_i = lax.broadcasted_iota(jnp.int32, (TQ, TQ), 1)
    nt_dims = (((1,), (1,)), ((), ()))

    def scores(i, e):
        q0 = i * TQ
        head = 2 * j + e
        q_t = q_ref[q0:q0 + TQ, :]
        qa_t = qaug_ref[q0:q0 + TQ, :]
        qm = jnp.where(lane_q // HEAD_DIM == e, q_t, jnp.zeros_like(q_t))
        in_head = (lane_q >= AUG_PER_HEAD * head) & (lane_q < AUG_PER_HEAD * (head + 1))
        qa = jnp.where(in_head, qa_t, jnp.zeros_like(qa_t))
        qcat = jnp.concatenate([qm, qa], axis=1)
        k_diag = jnp.concatenate([k_ref[q0:q0 + TQ, :], kaug_ref[q0:q0 + TQ, :]], axis=1)
        s_d = lax.dot_general(k_diag, qcat, nt_dims, preferred_element_type=F32)
        s_d = jnp.where(key_i <= qry_i, s_d, MASK_VALUE)
        m = jnp.max(s_d, axis=0, keepdims=True)
        s_o = None
        if i:
            k_off = jnp.concatenate([k_ref[0:q0, :], kaug_ref[0:q0, :]], axis=1)
            s_o = lax.dot_general(k_off, qcat, nt_dims, preferred_element_type=F32)
            m = jnp.maximum(m, jnp.max(s_o, axis=0, keepdims=True))
        return s_d, s_o, m

    def weighted_values(i, e, s_d, s_o, m):
        q0 = i * TQ
        rows = slice(e * VT_ROWS, (e + 1) * VT_ROWS)
        p_d = jnp.exp2((s_d - m).astype(BF16))
        o_t = jnp.dot(vt_ref[rows, q0:q0 + TQ], p_d, preferred_element_type=F32)
        if i:
            p_o = jnp.exp2((s_o - m).astype(BF16))
            o_t = o_t + jnp.dot(vt_ref[rows, 0:q0], p_o, preferred_element_type=F32)
        return o_t[0:HEAD_DIM, :] * (1.0 / o_t[HEAD_DIM:HEAD_DIM + 1, :])

    units = [(i, e) for i in range(n_q) for e in range(2)]
    y_t = {}
    pending = None
    for unit in units + [None]:
        cur = (unit, scores(*unit)) if unit is not None else None
        if pending is not None:
            (pi, pe), st = pending
            y_t[pe] = weighted_values(pi, pe, *st)
            if pe == 1:
                q0 = pi * TQ
                y = jnp.concatenate([y_t[0], y_t[1]], axis=0).T
                o_ref[q0:q0 + TQ, :] = (y * ga_ref[q0:q0 + TQ, :].astype(F32)).astype(BF16)
        pending = cur


def _attention(z, lf, bsz, seq):
    m = bsz * seq
    n_pairs = ATT_HEADS // 2
    col = lambda g: (lambda b, j: (b, g * n_pairs + j))
    blk = (seq, LANES)
    return pl.pallas_call(
        functools.partial(_attn_kernel, seq=seq),
        grid=(bsz, n_pairs),
        in_specs=[
            pl.BlockSpec(blk, col(0)),
            pl.BlockSpec(blk, col(1)),
            pl.BlockSpec(blk, col(2)),
            pl.BlockSpec(blk, col(3)),
            pl.BlockSpec(blk, lambda b, j: (b, 0)),
        ],
        out_specs=pl.BlockSpec(blk, lambda b, j: (b, j)),
        out_shape=jax.ShapeDtypeStruct((m, D_MODEL), BF16),
        scratch_shapes=[pltpu.VMEM(blk, BF16), pltpu.VMEM(blk, BF16),
                        pltpu.VMEM((2 * VT_ROWS, seq), BF16)],
        compiler_params=pltpu.CompilerParams(
            dimension_semantics=("arbitrary", "arbitrary"), vmem_limit_bytes=VMEM_LIMIT),
        name="fox_attention",
    )(z, z, z, z, lf)


def _rglru_kernel(xr_ref, gr_ref, cw_ref, cb_ref, wa_ref, ba_ref, wx_ref, bx_ref, lam_ref,
                  o_ref, xs_ref, a_ref, u_ref, hs_ref, h_ref):
    t = pl.program_id(1)
    halo = 8

    @pl.when(t == 0)
    def _reset():
        xs_ref[0:halo, :] = jnp.zeros((halo, D_MODEL), F32)
        h_ref[...] = jnp.zeros_like(h_ref)

    xs_ref[halo:halo + T_RNN, :] = xr_ref[...].astype(F32)
    xc = cb_ref[...] + cw_ref[0:1, :] * xs_ref[pl.ds(halo - 3, T_RNN), :]
    for tap in range(1, CONV_W):
        xc = xc + cw_ref[tap:tap + 1, :] * xs_ref[pl.ds(halo - 3 + tap, T_RNN), :]
    xs_ref[0:halo, :] = xs_ref[T_RNN:T_RNN + halo, :]

    xcb = xc.astype(BF16)
    sp = _softplus(-lam_ref[...])
    for kb in range(D_MODEL // MXU_DIM):
        cols = slice(kb * MXU_DIM, (kb + 1) * MXU_DIM)
        xk = xcb[:, cols]
        r = _sigmoid(jnp.dot(xk, wa_ref[kb], preferred_element_type=F32) + ba_ref[:, cols])
        ig = _sigmoid(jnp.dot(xk, wx_ref[kb], preferred_element_type=F32) + bx_ref[:, cols])
        log_a = -RG_C * r * sp[:, cols]
        a = jnp.exp(log_a)
        one_m_a2 = (1.0 + a * a) * jnp.tanh(-log_a)
        a_ref[:, cols] = a
        u_ref[:, cols] = jnp.sqrt(jnp.maximum(one_m_a2, 0.0)) * (ig * xc[:, cols])

    def step(s, h):
        h = a_ref[pl.ds(s, 1), :] * h + u_ref[pl.ds(s, 1), :]
        hs_ref[pl.ds(s, 1), :] = h
        return h

    h_ref[...] = lax.fori_loop(0, T_RNN, step, h_ref[...], unroll=8)
    o_ref[...] = (hs_ref[...] * gr_ref[...].astype(F32)).astype(BF16)


def _rglru(z, cw, cb, wa_bd, ba, wx_bd, bx, lam, bsz, seq):
    m = bsz * seq
    n_t = seq // T_RNN
    const2 = lambda b, t: (0, 0)
    const3 = lambda b, t: (0, 0, 0)
    n_kb = D_MODEL // MXU_DIM
    return pl.pallas_call(
        _rglru_kernel,
        grid=(bsz, n_t),
        in_specs=[
            pl.BlockSpec((T_RNN, D_MODEL), lambda b, t: (b * n_t + t, 4)),
            pl.BlockSpec((T_RNN, D_MODEL), lambda b, t: (b * n_t + t, 5)),
            pl.BlockSpec((CONV_W, D_MODEL), const2),
            pl.BlockSpec((1, D_MODEL), const2),
            pl.BlockSpec((n_kb, MXU_DIM, MXU_DIM), const3),
            pl.BlockSpec((1, D_MODEL), const2),
            pl.BlockSpec((n_kb, MXU_DIM, MXU_DIM), const3),
            pl.BlockSpec((1, D_MODEL), const2),
            pl.BlockSpec((1, D_MODEL), const2),
        ],
        out_specs=pl.BlockSpec((T_RNN, D_MODEL), lambda b, t: (b * n_t + t, 0)),
        out_shape=jax.ShapeDtypeStruct((m, D_MODEL), BF16),
        scratch_shapes=[
            pltpu.VMEM((T_RNN + 8, D_MODEL), F32),
            pltpu.VMEM((T_RNN, D_MODEL), F32),
            pltpu.VMEM((T_RNN, D_MODEL), F32),
            pltpu.VMEM((T_RNN, D_MODEL), F32),
            pltpu.VMEM((1, D_MODEL), F32),
        ],
        compiler_params=pltpu.CompilerParams(
            dimension_semantics=("arbitrary", "arbitrary"), vmem_limit_bytes=VMEM_LIMIT),
        name="rglru",
    )(z, z, cw, cb, wa_bd, ba, wx_bd, bx, lam)


def _out_kernel(x_ref, ya_ref, yr_ref, sa_ref, sr_ref, wa_ref, wr_ref, wo_ref, pw_ref, o_ref):
    pa = jnp.dot(ya_ref[...], wa_ref[...], preferred_element_type=F32)
    pr = jnp.dot(yr_ref[...], wr_ref[...], preferred_element_type=F32)
    mrg = sa_ref[...].astype(F32) * pa + sr_ref[...].astype(F32) * pr
    o = jnp.dot(mrg.astype(BF16), wo_ref[...], preferred_element_type=F32)
    var = jnp.mean(o * o, axis=-1, keepdims=True)
    o_ref[...] = x_ref[...] + o * lax.rsqrt(var + NORM_EPS) * pw_ref[...]


def _out_stage(x2, ya, yr, z, wa, wr, wo, pw):
    m = x2.shape[0]
    const = lambda i: (0, 0)
    row = lambda i: (i, 0)
    blk = (BM_OUT, D_MODEL)
    wblk = (D_MODEL, D_MODEL)
    return pl.pallas_call(
        _out_kernel,
        grid=(m // BM_OUT,),
        in_specs=[
            pl.BlockSpec(blk, row),
            pl.BlockSpec(blk, row),
            pl.BlockSpec(blk, row),
            pl.BlockSpec(blk, lambda i: (i, 6)),
            pl.BlockSpec(blk, lambda i: (i, 7)),
            pl.BlockSpec(wblk, const),
            pl.BlockSpec(wblk, const),
            pl.BlockSpec(wblk, const),
            pl.BlockSpec((1, D_MODEL), const),
        ],
        out_specs=pl.BlockSpec(blk, row),
        out_shape=jax.ShapeDtypeStruct((m, D_MODEL), F32),
        compiler_params=pltpu.CompilerParams(
            dimension_semantics=("arbitrary",), vmem_limit_bytes=VMEM_LIMIT),
        name="out_stage",
    )(x2, ya, yr, z, z, wa, wr, wo, pw)


def _block_diag(w):
    per = MXU_DIM // RNN_BLOCK_W
    n_kb = RNN_BLOCKS // per
    w4 = w.reshape(n_kb, per, RNN_BLOCK_W, RNN_BLOCK_W)
    eye = jnp.eye(per, dtype=w.dtype)
    bd = jnp.einsum('kgij,gh->kgihj', w4, eye)
    return bd.reshape(n_kb, MXU_DIM, MXU_DIM).astype(BF16)


def kernel(x, pre_norm_w, w_in, b_in, conv_w, conv_b, rg_wa, rg_ba, rg_wx, rg_bx, rg_lambda,
           w_branch_a, w_branch_r, w_out, post_norm_w):
    bsz, seq, d = x.shape
    assert d == D_MODEL and seq % TQ == 0 and seq % T_RNN == 0
    assert (bsz * seq) % BM_IN == 0 and (bsz * seq) % BM_OUT == 0
    d_att = ATT_HEADS * HEAD_DIM
    f0 = 3 * d_att
    f1 = f0 + ATT_HEADS
    x2 = x.reshape(bsz * seq, d)
    row = lambda v: v.reshape(1, -1)
    for l in range(pre_norm_w.shape[0]):
        w = w_in[l]
        w8 = jnp.concatenate([w[:, :f0], w[:, f1:]], axis=1).astype(BF16)
        b8 = row(jnp.concatenate([b_in[l, :f0], b_in[l, f1:]]))
        wf = jnp.pad(w[:, f0:f1], ((0, 0), (0, LANES - ATT_HEADS))).astype(BF16)
        bf = row(jnp.pad(b_in[l, f0:f1], (0, LANES - ATT_HEADS)))
        z, lf = _in_proj(x2, row(pre_norm_w[l]), w8, b8, wf, bf)
        ya = _attention(z, lf, bsz, seq)
        yr = _rglru(z, conv_w[l], row(conv_b[l]), _block_diag(rg_wa[l]), row(rg_ba[l]),
                    _block_diag(rg_wx[l]), row(rg_bx[l]), row(rg_lambda[l]), bsz, seq)
        x2 = _out_stage(x2, ya, yr, z, w_branch_a[l].astype(BF16), w_branch_r[l].astype(BF16),
                        w_out[l].astype(BF16), row(post_norm_w[l]))
    return x2.reshape(bsz, seq, d)
```

```python
import functools

import jax
import jax.numpy as jnp
from jax import lax
from jax.experimental import pallas as pl
from jax.experimental.pallas import tpu as pltpu

D_MODEL = 1024
ATT_HEADS = 16
HEAD_DIM = 64
RNN_BLOCKS = 16
RNN_BLOCK_W = 64
CONV_W = 4
RG_C = 8.0
NORM_EPS = 1e-6
MASK_VALUE = -1e30

LANES = 128
SUBLANES = 8
MXU_DIM = 256
N_GROUPS = 8
AUG_PER_HEAD = 6
VMEM_LIMIT = 56 * 1024 * 1024

BM_IN = 512
BM_OUT = 512
T_RNN = 256
RG_STRIDE = 8
TQ = 256
VT_PAD = 16
VT_ROWS = HEAD_DIM + VT_PAD
LOG2E = 1.4426950408889634

F32 = jnp.float32
BF16 = jnp.bfloat16


def _sigmoid(x):
    return 1.0 / (1.0 + jnp.exp(-x))


def _silu(x):
    return x * _sigmoid(x)


def _log_sigmoid(x):
    return jnp.minimum(x, 0.0) - jnp.log(1.0 + jnp.exp(-jnp.abs(x)))


def _softplus(x):
    return jnp.maximum(x, 0.0) + jnp.log(1.0 + jnp.exp(-jnp.abs(x)))


def _split3(x):
    hi = x.astype(BF16)
    r1 = x - hi.astype(F32)
    mid = r1.astype(BF16)
    lo = (r1 - mid.astype(F32)).astype(BF16)
    return hi, mid, lo


def _in_proj_kernel(x_ref, nw_ref, w_ref, b_ref, wf_ref, bf_ref, z_ref, lf_ref):
    x = x_ref[...]
    var = jnp.mean(x * x, axis=-1, keepdims=True)
    h = (x * lax.rsqrt(var + NORM_EPS) * nw_ref[...]).astype(BF16)
    for g in range(N_GROUPS):
        cols = slice(g * D_MODEL, (g + 1) * D_MODEL)
        acc = jnp.dot(h, w_ref[:, cols], preferred_element_type=F32) + b_ref[:, cols]
        if g == 0:
            val = acc * (HEAD_DIM ** -0.5 * LOG2E)
        elif g in (3, 5):
            val = _silu(acc)
        elif g in (6, 7):
            val = _sigmoid(acc)
        else:
            val = acc
        z_ref[:, cols] = val.astype(BF16)
    f = jnp.dot(h, wf_ref[...], preferred_element_type=F32) + bf_ref[...]
    lf_ref[...] = _log_sigmoid(f)


def _in_proj(x2, nw, w8, b8, wf, bf):
    m = x2.shape[0]
    const = lambda i: (0, 0)
    return pl.pallas_call(
        _in_proj_kernel,
        grid=(m // BM_IN,),
        in_specs=[
            pl.BlockSpec((BM_IN, D_MODEL), lambda i: (i, 0)),
            pl.BlockSpec((1, D_MODEL), const),
            pl.BlockSpec((D_MODEL, N_GROUPS * D_MODEL), const, pipeline_mode=pl.Buffered(1)),
            pl.BlockSpec((1, N_GROUPS * D_MODEL), const),
            pl.BlockSpec((D_MODEL, LANES), const),
            pl.BlockSpec((1, LANES), const),
        ],
        out_specs=[
            pl.BlockSpec((BM_IN, N_GROUPS * D_MODEL), lambda i: (i, 0)),
            pl.BlockSpec((BM_IN, LANES), lambda i: (i, 0)),
        ],
        out_shape=[
            jax.ShapeDtypeStruct((m, N_GROUPS * D_MODEL), BF16),
            jax.ShapeDtypeStruct((m, LANES), F32),
        ],
        compiler_params=pltpu.CompilerParams(
            dimension_semantics=("arbitrary",), vmem_limit_bytes=VMEM_LIMIT),
        name="in_proj",
    )(x2, nw, w8, b8, wf, bf)


def _attn_kernel(q_ref, k_ref, v_ref, ga_ref, lf_ref, o_ref, qaug_ref, kaug_ref, vt_ref, *, seq):
    j = pl.program_id(1)
    n_q = seq // TQ

    @pl.when(j == 0)
    def _build_aug():
        row = lax.broadcasted_iota(jnp.int32, (LANES, LANES), 0)
        col = lax.broadcasted_iota(jnp.int32, (LANES, LANES), 1)
        tri = (col <= row).astype(BF16)
        lane1 = lax.broadcasted_iota(jnp.int32, (1, LANES), 1)
        n_aug = ATT_HEADS * AUG_PER_HEAD
        k_ones = ((lane1 % AUG_PER_HEAD >= 3) & (lane1 < n_aug)).astype(F32)
        q_ones = ((lane1 % AUG_PER_HEAD < 3) & (lane1 < n_aug)).astype(F32)
        sel = [((col == AUG_PER_HEAD * row + r) & (row < ATT_HEADS)).astype(BF16)
               for r in range(AUG_PER_HEAD)]
        carry = jnp.zeros((1, LANES), F32)
        for blk in range(seq // LANES):
            rows = slice(blk * LANES, (blk + 1) * LANES)
            pieces = _split3(lf_ref[rows, :] * LOG2E)
            c = carry
            for p in pieces:
                c = c + jnp.dot(tri, p, preferred_element_type=F32)
            carry = c[LANES - 1:LANES, :]
            c3 = _split3(c)
            kacc = k_ones
            qacc = q_ones
            for r in range(3):
                kacc = kacc - jnp.dot(c3[r], sel[r], preferred_element_type=F32)
                qacc = qacc + jnp.dot(c3[r], sel[3 + r], preferred_element_type=F32)
            kaug_ref[rows, :] = kacc.astype(BF16)
            qaug_ref[rows, :] = qacc.astype(BF16)

    v_t = v_ref[...].astype(F32).T.astype(BF16)
    one_row = (lax.broadcasted_iota(jnp.int32, (VT_PAD, seq), 0) == 0).astype(BF16)
    for e in range(2):
        vt_ref[e * VT_ROWS:e * VT_ROWS + HEAD_DIM, :] = v_t[e * HEAD_DIM:(e + 1) * HEAD_DIM, :]
        vt_ref[e * VT_ROWS + HEAD_DIM:(e + 1) * VT_ROWS, :] = one_row
    lane_q = lax.broadcasted_iota(jnp.int32, (TQ, LANES), 1)
    key_i = lax.broadcasted_iota(jnp.int32, (TQ, TQ), 0)
    qry_i = lax.broadcasted_iota(jnp.int32, (TQ, TQ), 1)
    nt_dims = (((1,), (1,)), ((), ()))

    def scores(i, e):
        q0 = i * TQ
        head = 2 * j + e
        q_t = q_ref[q0:q0 + TQ, :]
        qa_t = qaug_ref[q0:q0 + TQ, :]
        qm = jnp.where(lane_q // HEAD_DIM == e, q_t, jnp.zeros_like(q_t))
        in_head = (lane_q >= AUG_PER_HEAD * head) & (lane_q < AUG_PER_HEAD * (head + 1))
        qa = jnp.where(in_head, qa_t, jnp.zeros_like(qa_t))
        qcat = jnp.concatenate([qm, qa], axis=1)
        k_diag = jnp.concatenate([k_ref[q0:q0 + TQ, :], kaug_ref[q0:q0 + TQ, :]], axis=1)
        s_d = lax.dot_general(k_diag, qcat, nt_dims, preferred_element_type=F32)
        s_d = jnp.where(key_i <= qry_i, s_d, MASK_VALUE)
        m = jnp.max(s_d, axis=0, keepdims=True)
        s_o = None
        if i:
            k_off = jnp.concatenate([k_ref[0:q0, :], kaug_ref[0:q0, :]], axis=1)
            s_o = lax.dot_general(k_off, qcat, nt_dims, preferred_element_type=F32)
            m = jnp.maximum(m, jnp.max(s_o, axis=0, keepdims=True))
        return s_d, s_o, m

    def weighted_values(i, e, s_d, s_o, m):
        q0 = i * TQ
        rows = slice(e * VT_ROWS, (e + 1) * VT_ROWS)
        p_d = jnp.exp2((s_d - m).astype(BF16))
        o_t = jnp.dot(vt_ref[rows, q0:q0 + TQ], p_d, preferred_element_type=F32)
        if i:
            p_o = jnp.exp2((s_o - m).astype(BF16))
            o_t = o_t + jnp.dot(vt_ref[rows, 0:q0], p_o, preferred_element_type=F32)
        return o_t[0:HEAD_DIM, :] * (1.0 / o_t[HEAD_DIM:HEAD_DIM + 1, :])

    units = [(i, e) for i in range(n_q) for e in range(2)]
    y_t = {}
    pending = None
    for unit in units + [None]:
        cur = (unit, scores(*unit)) if unit is not None else None
        if pending is not None:
            (pi, pe), st = pending
            y_t[pe] = weighted_values(pi, pe, *st)
            if pe == 1:
                q0 = pi * TQ
                y = jnp.concatenate([y_t[0], y_t[1]], axis=0).T
                o_ref[q0:q0 + TQ, :] = (y * ga_ref[q0:q0 + TQ, :].astype(F32)).astype(BF16)
        pending = cur


def _attention(z, lf, bsz, seq):
    m = bsz * seq
    n_pairs = ATT_HEADS // 2
    col = lambda g: (lambda b, j: (b, g * n_pairs + j))
    blk = (seq, LANES)
    return pl.pallas_call(
        functools.partial(_attn_kernel, seq=seq),
        grid=(bsz, n_pairs),
        in_specs=[
            pl.BlockSpec(blk, col(0)),
            pl.BlockSpec(blk, col(1)),
            pl.BlockSpec(blk, col(2)),
            pl.BlockSpec(blk, col(3)),
            pl.BlockSpec(blk, lambda b, j: (b, 0)),
        ],
        out_specs=pl.BlockSpec(blk, lambda b, j: (b, j)),
        out_shape=jax.ShapeDtypeStruct((m, D_MODEL), BF16),
        scratch_shapes=[pltpu.VMEM(blk, BF16), pltpu.VMEM(blk, BF16),
                        pltpu.VMEM((2 * VT_ROWS, seq), BF16)],
        compiler_params=pltpu.CompilerParams(
            dimension_semantics=("arbitrary", "arbitrary"), vmem_limit_bytes=VMEM_LIMIT),
        name="fox_attention",
    )(z, z, z, z, lf)


def _shift_rows(x3, d, fill):
    n_tiles, _, width = x3.shape
    fill = jnp.broadcast_to(jnp.asarray(fill, x3.dtype), (1, SUBLANES, width))
    if d % SUBLANES == 0:
        dt = d // SUBLANES
        return jnp.concatenate([jnp.broadcast_to(fill, (dt, SUBLANES, width)),
                                x3[:n_tiles - dt]], axis=0)
    sub = lax.broadcasted_iota(jnp.int32, (1, SUBLANES, 1), 1)
    prev = jnp.concatenate([fill, x3[:n_tiles - 1]], axis=0)
    return jnp.where(sub >= d, pltpu.roll(x3, d, axis=1), pltpu.roll(prev, d, axis=1))


def _rglru_kernel(xr_ref, gr_ref, perm_ref, unperm_ref, cw_ref, cb_ref, wa_ref, ba_ref, wx_ref,
                  bx_ref, lam_ref, o_ref, xs_ref, h_ref):
    t = pl.program_id(1)
    n_grp = T_RNN // RG_STRIDE
    hist = SUBLANES

    @pl.when(t == 0)
    def _reset():
        xs_ref[:, 0:hist, :] = jnp.zeros((CONV_W - 1, hist, D_MODEL), F32)
        h_ref[...] = jnp.zeros_like(h_ref)

    blk = lambda v, r: v[r * n_grp:(r + 1) * n_grp]
    xp = jnp.dot(perm_ref[...], xr_ref[...], preferred_element_type=F32)
    grp = jnp.dot(perm_ref[...], gr_ref[...], preferred_element_type=F32)
    first = RG_STRIDE - CONV_W + 1
    x_prev = {}
    for r in range(first, RG_STRIDE):
        xs_ref[r - first, hist:hist + n_grp, :] = blk(xp, r)
        x_prev[r] = xs_ref[r - first, pl.ds(hist - 1, n_grp), :]
        xs_ref[r - first, 0:hist, :] = xs_ref[r - first, n_grp:n_grp + hist, :]
    src = lambda r: blk(xp, r) if r >= 0 else x_prev[r + RG_STRIDE]
    xc_blocks = []
    for r in range(RG_STRIDE):
        acc = cb_ref[...] + cw_ref[0:1, :] * src(r - CONV_W + 1)
        for tap in range(1, CONV_W):
            acc = acc + cw_ref[tap:tap + 1, :] * src(r - CONV_W + 1 + tap)
        xc_blocks.append(acc)
    xc = jnp.concatenate(xc_blocks, axis=0)

    xcb = xc.astype(BF16)
    half_c_sp = (0.5 * RG_C) * _softplus(-lam_ref[...])
    tiles = n_grp // SUBLANES
    for kb in range(D_MODEL // MXU_DIM):
        cols = slice(kb * MXU_DIM, (kb + 1) * MXU_DIM)
        xk = xcb[:, cols]
        t_r = jnp.tanh(jnp.dot(xk, wa_ref[kb], preferred_element_type=F32) + ba_ref[:, cols])
        t_i = jnp.tanh(jnp.dot(xk, wx_ref[kb], preferred_element_type=F32) + bx_ref[:, cols])
        neg_log_a = half_c_sp[:, cols] + half_c_sp[:, cols] * t_r
        a = jnp.exp(-neg_log_a)
        one_m_a2 = (1.0 + a * a) * jnp.tanh(neg_log_a)
        mult = jnp.where(one_m_a2 > 0.0, one_m_a2 * lax.rsqrt(one_m_a2), 0.0)
        u = mult * ((0.5 + 0.5 * t_i) * xc[:, cols])

        a_in = [blk(a, 0)]
        h_in = [blk(u, 0)]
        for r in range(1, RG_STRIDE):
            h_in.append(blk(a, r) * h_in[-1] + blk(u, r))
            a_in.append(blk(a, r) * a_in[-1])
        a_g = a_in[-1].reshape(tiles, SUBLANES, MXU_DIM)
        h_g = h_in[-1].reshape(tiles, SUBLANES, MXU_DIM)
        d = 1
        while d < n_grp:
            h_g = h_g + a_g * _shift_rows(h_g, d, 0.0)
            a_g = a_g * _shift_rows(a_g, d, 1.0)
            d *= 2
        h0 = h_ref[:, cols]
        h_end = h_g + a_g * h0
        h_ent = _shift_rows(h_end, 1, h0).reshape(n_grp, MXU_DIM)
        h_ref[:, cols] = h_end[tiles - 1, SUBLANES - 1:SUBLANES, :]
        y = jnp.concatenate(
            [(h_in[r] + a_in[r] * h_ent) * blk(grp, r)[:, cols] for r in range(RG_STRIDE)], axis=0)
        o_ref[:, cols] = jnp.dot(unperm_ref[...], y.astype(BF16),
                                 preferred_element_type=F32).astype(BF16)


def _rglru(z, cw, cb, wa_bd, ba, wx_bd, bx, lam, bsz, seq):
    m = bsz * seq
    n_t = seq // T_RNN
    const2 = lambda b, t: (0, 0)
    const3 = lambda b, t: (0, 0, 0)
    n_kb = D_MODEL // MXU_DIM
    n_grp = T_RNN // RG_STRIDE
    pos = jnp.arange(T_RNN)
    step_of_row = (pos % n_grp) * RG_STRIDE + pos // n_grp
    perm = (step_of_row[:, None] == pos[None, :]).astype(BF16)
    return pl.pallas_call(
        _rglru_kernel,
        grid=(bsz, n_t),
        in_specs=[
            pl.BlockSpec((T_RNN, D_MODEL), lambda b, t: (b * n_t + t, 4)),
            pl.BlockSpec((T_RNN, D_MODEL), lambda b, t: (b * n_t + t, 5)),
            pl.BlockSpec((T_RNN, T_RNN), const2),
            pl.BlockSpec((T_RNN, T_RNN), const2),
            pl.BlockSpec((CONV_W, D_MODEL), const2),
            pl.BlockSpec((1, D_MODEL), const2),
            pl.BlockSpec((n_kb, MXU_DIM, MXU_DIM), const3),
            pl.BlockSpec((1, D_MODEL), const2),
            pl.BlockSpec((n_kb, MXU_DIM, MXU_DIM), const3),
            pl.BlockSpec((1, D_MODEL), const2),
            pl.BlockSpec((1, D_MODEL), const2),
        ],
        out_specs=pl.BlockSpec((T_RNN, D_MODEL), lambda b, t: (b * n_t + t, 0)),
        out_shape=jax.ShapeDtypeStruct((m, D_MODEL), BF16),
        scratch_shapes=[
            pltpu.VMEM((CONV_W - 1, SUBLANES + n_grp, D_MODEL), F32),
            pltpu.VMEM((1, D_MODEL), F32),
        ],
        compiler_params=pltpu.CompilerParams(
            dimension_semantics=("arbitrary", "arbitrary"), vmem_limit_bytes=VMEM_LIMIT),
        name="rglru",
    )(z, z, perm, perm.T, cw, cb, wa_bd, ba, wx_bd, bx, lam)


def _out_kernel(x_ref, ya_ref, yr_ref, sa_ref, sr_ref, wa_ref, wr_ref, wo_ref, pw_ref, o_ref):
    pa = jnp.dot(ya_ref[...], wa_ref[...], preferred_element_type=F32)
    pr = jnp.dot(yr_ref[...], wr_ref[...], preferred_element_type=F32)
    mrg = sa_ref[...].astype(F32) * pa + sr_ref[...].astype(F32) * pr
    o = jnp.dot(mrg.astype(BF16), wo_ref[...], preferred_element_type=F32)
    var = jnp.mean(o * o, axis=-1, keepdims=True)
    o_ref[...] = x_ref[...] + o * lax.rsqrt(var + NORM_EPS) * pw_ref[...]


def _out_stage(x2, ya, yr, z, wa, wr, wo, pw):
    m = x2.shape[0]
    const = lambda i: (0, 0)
    row = lambda i: (i, 0)
    blk = (BM_OUT, D_MODEL)
    wblk = (D_MODEL, D_MODEL)
    return pl.pallas_call(
        _out_kernel,
        grid=(m // BM_OUT,),
        in_specs=[
            pl.BlockSpec(blk, row),
            pl.BlockSpec(blk, row),
            pl.BlockSpec(blk, row),
            pl.BlockSpec(blk, lambda i: (i, 6)),
            pl.BlockSpec(blk, lambda i: (i, 7)),
            pl.BlockSpec(wblk, const),
            pl.BlockSpec(wblk, const),
            pl.BlockSpec(wblk, const),
            pl.BlockSpec((1, D_MODEL), const),
        ],
        out_specs=pl.BlockSpec(blk, row),
        out_shape=jax.ShapeDtypeStruct((m, D_MODEL), F32),
        compiler_params=pltpu.CompilerParams(
            dimension_semantics=("arbitrary",), vmem_limit_bytes=VMEM_LIMIT),
        name="out_stage",
    )(x2, ya, yr, z, z, wa, wr, wo, pw)


def _block_diag(w):
    per = MXU_DIM // RNN_BLOCK_W
    n_kb = RNN_BLOCKS // per
    w4 = w.reshape(n_kb, per, RNN_BLOCK_W, RNN_BLOCK_W)
    eye = jnp.eye(per, dtype=w.dtype)
    bd = jnp.einsum('kgij,gh->kgihj', w4, eye)
    return bd.reshape(n_kb, MXU_DIM, MXU_DIM).astype(BF16)


def kernel(x, pre_norm_w, w_in, b_in, conv_w, conv_b, rg_wa, rg_ba, rg_wx, rg_bx, rg_lambda,
           w_branch_a, w_branch_r, w_out, post_norm_w):
    bsz, seq, d = x.shape
    assert d == D_MODEL and seq % TQ == 0 and seq % T_RNN == 0
    assert (bsz * seq) % BM_IN == 0 and (bsz * seq) % BM_OUT == 0
    d_att = ATT_HEADS * HEAD_DIM
    f0 = 3 * d_att
    f1 = f0 + ATT_HEADS
    x2 = x.reshape(bsz * seq, d)
    row = lambda v: v.reshape(1, -1)
    for l in range(pre_norm_w.shape[0]):
        w = w_in[l]
        w8 = jnp.concatenate([w[:, :f0], w[:, f1:]], axis=1).astype(BF16)
        b8 = row(jnp.concatenate([b_in[l, :f0], b_in[l, f1:]]))
        wf = jnp.pad(w[:, f0:f1], ((0, 0), (0, LANES - ATT_HEADS))).astype(BF16)
        bf = row(jnp.pad(b_in[l, f0:f1], (0, LANES - ATT_HEADS)))
        z, lf = _in_proj(x2, row(pre_norm_w[l]), w8, b8, wf, bf)
        ya = _attention(z, lf, bsz, seq)
        yr = _rglru(z, conv_w[l], row(conv_b[l]), _block_diag(0.5 * rg_wa[l]),
                    row(0.5 * rg_ba[l]), _block_diag(0.5 * rg_wx[l]), row(0.5 * rg_bx[l]),
                    row(rg_lambda[l]), bsz, seq)
        x2 = _out_stage(x2, ya, yr, z, w_branch_a[l].astype(BF16), w_branch_r[l].astype(BF16),
                        w_out[l].astype(BF16), row(post_norm_w[l]))
    return x2.reshape(bsz, seq, d)
```

```python
import functools

import jax
import jax.numpy as jnp
from jax import lax
from jax.experimental import pallas as pl
from jax.experimental.pallas import tpu as pltpu

D_MODEL = 1024
ATT_HEADS = 16
HEAD_DIM = 64
RNN_BLOCKS = 16
RNN_BLOCK_W = 64
CONV_W = 4
RG_C = 8.0
NORM_EPS = 1e-6
MASK_VALUE = -1e30

LANES = 128
SUBLANES = 8
MXU_DIM = 256
N_GROUPS = 8
AUG_PER_HEAD = 6
VMEM_LIMIT = 56 * 1024 * 1024

BM_IN = 512
BM_OUT = 1024
OUT_SUBTILES = 2
T_RNN = 512
RG_STRIDE = 8
TQ = 256
VT_PAD = 16
VT_ROWS = HEAD_DIM + VT_PAD
LOG2E = 1.4426950408889634

F32 = jnp.float32
BF16 = jnp.bfloat16


def _sigmoid(x):
    return 1.0 / (1.0 + jnp.exp(-x))


def _silu(x):
    return x * _sigmoid(x)


def _log_sigmoid(x):
    return jnp.minimum(x, 0.0) - jnp.log(1.0 + jnp.exp(-jnp.abs(x)))


def _softplus(x):
    return jnp.maximum(x, 0.0) + jnp.log(1.0 + jnp.exp(-jnp.abs(x)))


def _split3(x):
    hi = x.astype(BF16)
    r1 = x - hi.astype(F32)
    mid = r1.astype(BF16)
    lo = (r1 - mid.astype(F32)).astype(BF16)
    return hi, mid, lo


def _in_proj_kernel(x_ref, nw_ref, w_ref, b_ref, wf_ref, bf_ref, z_ref, lf_ref):
    x = x_ref[...]
    var = jnp.mean(x * x, axis=-1, keepdims=True)
    h = (x * lax.rsqrt(var + NORM_EPS) * nw_ref[...]).astype(BF16)
    for g in range(N_GROUPS):
        cols = slice(g * D_MODEL, (g + 1) * D_MODEL)
        acc = jnp.dot(h, w_ref[:, cols], preferred_element_type=F32) + b_ref[:, cols]
        if g == 0:
            val = acc * (HEAD_DIM ** -0.5 * LOG2E)
        elif g in (3, 5):
            val = _silu(acc)
        elif g in (6, 7):
            val = _sigmoid(acc)
        else:
            val = acc
        z_ref[:, cols] = val.astype(BF16)
    f = jnp.dot(h, wf_ref[...], preferred_element_type=F32) + bf_ref[...]
    lf_ref[...] = _log_sigmoid(f)


def _in_proj(x2, nw, w8, b8, wf, bf):
    m = x2.shape[0]
    const = lambda i: (0, 0)
    return pl.pallas_call(
        _in_proj_kernel,
        grid=(m // BM_IN,),
        in_specs=[
            pl.BlockSpec((BM_IN, D_MODEL), lambda i: (i, 0)),
            pl.BlockSpec((1, D_MODEL), const),
            pl.BlockSpec((D_MODEL, N_GROUPS * D_MODEL), const, pipeline_mode=pl.Buffered(1)),
            pl.BlockSpec((1, N_GROUPS * D_MODEL), const),
            pl.BlockSpec((D_MODEL, LANES), const),
            pl.BlockSpec((1, LANES), const),
        ],
        out_specs=[
            pl.BlockSpec((BM_IN, N_GROUPS * D_MODEL), lambda i: (i, 0)),
            pl.BlockSpec((BM_IN, LANES), lambda i: (i, 0)),
        ],
        out_shape=[
            jax.ShapeDtypeStruct((m, N_GROUPS * D_MODEL), BF16),
            jax.ShapeDtypeStruct((m, LANES), F32),
        ],
        compiler_params=pltpu.CompilerParams(
            dimension_semantics=("arbitrary",), vmem_limit_bytes=VMEM_LIMIT),
        name="in_proj",
    )(x2, nw, w8, b8, wf, bf)


def _attn_kernel(q_ref, k_ref, v_ref, ga_ref, lf_ref, o_ref, qaug_ref, kaug_ref, vt_ref, *, seq):
    j = pl.program_id(1)
    n_q = seq // TQ

    @pl.when(j == 0)
    def _build_aug():
        row = lax.broadcasted_iota(jnp.int32, (LANES, LANES), 0)
        col = lax.broadcasted_iota(jnp.int32, (LANES, LANES), 1)
        tri = (col <= row).astype(BF16)
        lane1 = lax.broadcasted_iota(jnp.int32, (1, LANES), 1)
        n_aug = ATT_HEADS * AUG_PER_HEAD
        k_ones = ((lane1 % AUG_PER_HEAD >= 3) & (lane1 < n_aug)).astype(F32)
        q_ones = ((lane1 % AUG_PER_HEAD < 3) & (lane1 < n_aug)).astype(F32)
        sel = [((col == AUG_PER_HEAD * row + r) & (row < ATT_HEADS)).astype(BF16)
               for r in range(AUG_PER_HEAD)]
        n_blk = seq // LANES
        local = []
        for blk in range(n_blk):
            pieces = _split3(lf_ref[blk * LANES:(blk + 1) * LANES, :] * LOG2E)
            acc = jnp.dot(tri, pieces[0], preferred_element_type=F32)
            for p in pieces[1:]:
                acc = acc + jnp.dot(tri, p, preferred_element_type=F32)
            local.append(acc)
        carry = jnp.zeros((1, LANES), F32)
        for blk in range(n_blk):
            rows = slice(blk * LANES, (blk + 1) * LANES)
            c = local[blk] + carry
            carry = c[LANES - 1:LANES, :]
            c3 = _split3(c)
            kacc = k_ones
            qacc = q_ones
            for r in range(3):
                kacc = kacc - jnp.dot(c3[r], sel[r], preferred_element_type=F32)
                qacc = qacc + jnp.dot(c3[r], sel[3 + r], preferred_element_type=F32)
            kaug_ref[rows, :] = kacc.astype(BF16)
            qaug_ref[rows, :] = qacc.astype(BF16)

    v_t = v_ref[...].astype(F32).T.astype(BF16)
    one_row = (lax.broadcasted_iota(jnp.int32, (VT_PAD, seq), 0) == 0).astype(BF16)
    for e in range(2):
        vt_ref[e * VT_ROWS:e * VT_ROWS + HEAD_DIM, :] = v_t[e * HEAD_DIM:(e + 1) * HEAD_DIM, :]
        vt_ref[e * VT_ROWS + HEAD_DIM:(e + 1) * VT_ROWS, :] = one_row
    lane_q = lax.broadcasted_iota(jnp.int32, (TQ, LANES), 1)
    key_i = lax.broadcasted_iota(jnp.int32, (TQ, TQ), 0)
    qry_i = lax.broadcasted_iota(jnp.int32, (TQ, TQ), 1)
    nt_dims = (((1,), (1,)), ((), ()))

    def scores(i, e):
        q0 = i * TQ
        head = 2 * j + e
        q_t = q_ref[q0:q0 + TQ, :]
        qa_t = qaug_ref[q0:q0 + TQ, :]
        qm = jnp.where(lane_q // HEAD_DIM == e, q_t, jnp.zeros_like(q_t))
        in_head = (lane_q >= AUG_PER_HEAD * head) & (lane_q < AUG_PER_HEAD * (head + 1))
        qa = jnp.where(in_head, qa_t, jnp.zeros_like(qa_t))
        qcat = jnp.concatenate([qm, qa], axis=1)
        k_diag = jnp.concatenate([k_ref[q0:q0 + TQ, :], kaug_ref[q0:q0 + TQ, :]], axis=1)
        s_d = lax.dot_general(k_diag, qcat, nt_dims, preferred_element_type=F32)
        s_d = jnp.where(key_i <= qry_i, s_d, MASK_VALUE)
        m = jnp.max(s_d, axis=0, keepdims=True)
        s_o = None
        if i:
            k_off = jnp.concatenate([k_ref[0:q0, :], kaug_ref[0:q0, :]], axis=1)
            s_o = lax.dot_general(k_off, qcat, nt_dims, preferred_element_type=F32)
            m = jnp.maximum(m, jnp.max(s_o, axis=0, keepdims=True))
        return s_d, s_o, m

    def weighted_values(i, e, s_d, s_o, m):
        q0 = i * TQ
        rows = slice(e * VT_ROWS, (e + 1) * VT_ROWS)
        p_d = jnp.exp2((s_d - m).astype(BF16))
        o_t = jnp.dot(vt_ref[rows, q0:q0 + TQ], p_d, preferred_element_type=F32)
        if i:
            p_o = jnp.exp2((s_o - m).astype(BF16))
            o_t = o_t + jnp.dot(vt_ref[rows, 0:q0], p_o, preferred_element_type=F32)
        return o_t[0:HEAD_DIM, :] * (1.0 / o_t[HEAD_DIM:HEAD_DIM + 1, :])

    pending = None
    for i in list(range(n_q)) + [None]:
        cur = [scores(i, e) for e in range(2)] if i is not None else None
        if pending is not None:
            pi, stats = pending
            y_t = [weighted_values(pi, e, *stats[e]) for e in range(2)]
            q0 = pi * TQ
            y = jnp.concatenate(y_t, axis=0).T
            o_ref[q0:q0 + TQ, :] = (y * ga_ref[q0:q0 + TQ, :].astype(F32)).astype(BF16)
        pending = (i, cur)


def _attention(z, lf, bsz, seq):
    m = bsz * seq
    n_pairs = ATT_HEADS // 2
    col = lambda g: (lambda b, j: (b, g * n_pairs + j))
    blk = (seq, LANES)
    return pl.pallas_call(
        functools.partial(_attn_kernel, seq=seq),
        grid=(bsz, n_pairs),
        in_specs=[
            pl.BlockSpec(blk, col(0)),
            pl.BlockSpec(blk, col(1)),
            pl.BlockSpec(blk, col(2)),
            pl.BlockSpec(blk, col(3)),
            pl.BlockSpec(blk, lambda b, j: (b, 0)),
        ],
        out_specs=pl.BlockSpec(blk, lambda b, j: (b, j)),
        out_shape=jax.ShapeDtypeStruct((m, D_MODEL), BF16),
        scratch_shapes=[pltpu.VMEM(blk, BF16), pltpu.VMEM(blk, BF16),
                        pltpu.VMEM((2 * VT_ROWS, seq), BF16)],
        compiler_params=pltpu.CompilerParams(
            dimension_semantics=("arbitrary", "arbitrary"), vmem_limit_bytes=VMEM_LIMIT),
        name="fox_attention",
    )(z, z, z, z, lf)


def _shift_rows(x3, d, fill):
    n_tiles, _, width = x3.shape
    fill = jnp.broadcast_to(jnp.asarray(fill, x3.dtype), (1, SUBLANES, width))
    if d % SUBLANES == 0:
        dt = d // SUBLANES
        return jnp.concatenate([jnp.broadcast_to(fill, (dt, SUBLANES, width)),
                                x3[:n_tiles - dt]], axis=0)
    sub = lax.broadcasted_iota(jnp.int32, (1, SUBLANES, 1), 1)
    prev = jnp.concatenate([fill, x3[:n_tiles - 1]], axis=0)
    return jnp.where(sub >= d, pltpu.roll(x3, d, axis=1), pltpu.roll(prev, d, axis=1))


def _rglru_kernel(xr_ref, gr_ref, perm_ref, unperm_ref, cw_ref, cb_ref, wa_ref, ba_ref, wx_ref,
                  bx_ref, lam_ref, o_ref, xs_ref, h_ref):
    t = pl.program_id(1)
    n_grp = T_RNN // RG_STRIDE
    hist = SUBLANES

    @pl.when(t == 0)
    def _reset():
        xs_ref[:, 0:hist, :] = jnp.zeros((CONV_W - 1, hist, D_MODEL), F32)
        h_ref[...] = jnp.zeros_like(h_ref)

    blk = lambda v, r: v[r * n_grp:(r + 1) * n_grp]
    xp = jnp.dot(perm_ref[...], xr_ref[...], preferred_element_type=F32)
    grp = jnp.dot(perm_ref[...], gr_ref[...], preferred_element_type=F32)
    first = RG_STRIDE - CONV_W + 1
    x_prev = {}
    for r in range(first, RG_STRIDE):
        xs_ref[r - first, hist:hist + n_grp, :] = blk(xp, r)
        x_prev[r] = xs_ref[r - first, pl.ds(hist - 1, n_grp), :]
        xs_ref[r - first, 0:hist, :] = xs_ref[r - first, n_grp:n_grp + hist, :]
    src = lambda r: blk(xp, r) if r >= 0 else x_prev[r + RG_STRIDE]
    xc_blocks = []
    for r in range(RG_STRIDE):
        acc = cb_ref[...] + cw_ref[0:1, :] * src(r - CONV_W + 1)
        for tap in range(1, CONV_W):
            acc = acc + cw_ref[tap:tap + 1, :] * src(r - CONV_W + 1 + tap)
        xc_blocks.append(acc)
    xc = jnp.concatenate(xc_blocks, axis=0)

    xcb = xc.astype(BF16)
    half_c_sp = (0.5 * RG_C) * _softplus(-lam_ref[...])
    tiles = n_grp // SUBLANES
    for kb in range(D_MODEL // MXU_DIM):
        cols = slice(kb * MXU_DIM, (kb + 1) * MXU_DIM)
        xk = xcb[:, cols]
        t_r = jnp.tanh(jnp.dot(xk, wa_ref[kb], preferred_element_type=F32) + ba_ref[:, cols])
        t_i = jnp.tanh(jnp.dot(xk, wx_ref[kb], preferred_element_type=F32) + bx_ref[:, cols])
        neg_log_a = half_c_sp[:, cols] + half_c_sp[:, cols] * t_r
        a = jnp.exp(-neg_log_a)
        one_m_a2 = (1.0 + a * a) * jnp.tanh(neg_log_a)
        mult = jnp.where(one_m_a2 > 0.0, one_m_a2 * lax.rsqrt(one_m_a2), 0.0)
        u = mult * ((0.5 + 0.5 * t_i) * xc[:, cols])

        a_in = [blk(a, 0)]
        h_in = [blk(u, 0)]
        for r in range(1, RG_STRIDE):
            h_in.append(blk(a, r) * h_in[-1] + blk(u, r))
            a_in.append(blk(a, r) * a_in[-1])
        a_g = a_in[-1].reshape(tiles, SUBLANES, MXU_DIM)
        h_g = h_in[-1].reshape(tiles, SUBLANES, MXU_DIM)
        d = 1
        while d < n_grp:
            h_g = h_g + a_g * _shift_rows(h_g, d, 0.0)
            a_g = a_g * _shift_rows(a_g, d, 1.0)
            d *= 2
        h0 = h_ref[:, cols]
        h_end = h_g + a_g * h0
        h_ent = _shift_rows(h_end, 1, h0).reshape(n_grp, MXU_DIM)
        h_ref[:, cols] = h_end[tiles - 1, SUBLANES - 1:SUBLANES, :]
        y = jnp.concatenate(
            [(h_in[r] + a_in[r] * h_ent) * blk(grp, r)[:, cols] for r in range(RG_STRIDE)], axis=0)
        o_ref[:, cols] = jnp.dot(unperm_ref[...], y.astype(BF16),
                                 preferred_element_type=F32).astype(BF16)


def _rglru(z, cw, cb, wa_bd, ba, wx_bd, bx, lam, bsz, seq):
    m = bsz * seq
    n_t = seq // T_RNN
    const2 = lambda b, t: (0, 0)
    const3 = lambda b, t: (0, 0, 0)
    n_kb = D_MODEL // MXU_DIM
    n_grp = T_RNN // RG_STRIDE
    pos = jnp.arange(T_RNN)
    step_of_row = (pos % n_grp) * RG_STRIDE + pos // n_grp
    perm = (step_of_row[:, None] == pos[None, :]).astype(BF16)
    return pl.pallas_call(
        _rglru_kernel,
        grid=(bsz, n_t),
        in_specs=[
            pl.BlockSpec((T_RNN, D_MODEL), lambda b, t: (b * n_t + t, 4)),
            pl.BlockSpec((T_RNN, D_MODEL), lambda b, t: (b * n_t + t, 5)),
            pl.BlockSpec((T_RNN, T_RNN), const2),
            pl.BlockSpec((T_RNN, T_RNN), const2),
            pl.BlockSpec((CONV_W, D_MODEL), const2),
            pl.BlockSpec((1, D_MODEL), const2),
            pl.BlockSpec((n_kb, MXU_DIM, MXU_DIM), const3),
            pl.BlockSpec((1, D_MODEL), const2),
            pl.BlockSpec((n_kb, MXU_DIM, MXU_DIM), const3),
            pl.BlockSpec((1, D_MODEL), const2),
            pl.BlockSpec((1, D_MODEL), const2),
        ],
        out_specs=pl.BlockSpec((T_RNN, D_MODEL), lambda b, t: (b * n_t + t, 0)),
        out_shape=jax.ShapeDtypeStruct((m, D_MODEL), BF16),
        scratch_shapes=[
            pltpu.VMEM((CONV_W - 1, SUBLANES + n_grp, D_MODEL), F32),
            pltpu.VMEM((1, D_MODEL), F32),
        ],
        compiler_params=pltpu.CompilerParams(
            dimension_semantics=("arbitrary", "arbitrary"), vmem_limit_bytes=VMEM_LIMIT),
        name="rglru",
    )(z, z, perm, perm.T, cw, cb, wa_bd, ba, wx_bd, bx, lam)


def _out_kernel(x_ref, ya_ref, yr_ref, sa_ref, sr_ref, wa_ref, wr_ref, wo_ref, pw_ref, o_ref):
    sub_rows = BM_OUT // OUT_SUBTILES
    for part in range(OUT_SUBTILES):
        rows = slice(part * sub_rows, (part + 1) * sub_rows)
        pa = jnp.dot(ya_ref[rows, :], wa_ref[...], preferred_element_type=F32)
        pr = jnp.dot(yr_ref[rows, :], wr_ref[...], preferred_element_type=F32)
        mrg = sa_ref[rows, :].astype(F32) * pa + sr_ref[rows, :].astype(F32) * pr
        o = jnp.dot(mrg.astype(BF16), wo_ref[...], preferred_element_type=F32)
        var = jnp.mean(o * o, axis=-1, keepdims=True)
        o_ref[rows, :] = x_ref[rows, :] + o * lax.rsqrt(var + NORM_EPS) * pw_ref[...]


def _out_stage(x2, ya, yr, z, wa, wr, wo, pw):
    m = x2.shape[0]
    const = lambda i: (0, 0)
    row = lambda i: (i, 0)
    blk = (BM_OUT, D_MODEL)
    wblk = (D_MODEL, D_MODEL)
    return pl.pallas_call(
        _out_kernel,
        grid=(m // BM_OUT,),
        in_specs=[
            pl.BlockSpec(blk, row),
            pl.BlockSpec(blk, row),
            pl.BlockSpec(blk, row),
            pl.BlockSpec(blk, lambda i: (i, 6)),
            pl.BlockSpec(blk, lambda i: (i, 7)),
            pl.BlockSpec(wblk, const),
            pl.BlockSpec(wblk, const),
            pl.BlockSpec(wblk, const),
            pl.BlockSpec((1, D_MODEL), const),
        ],
        out_specs=pl.BlockSpec(blk, row),
        out_shape=jax.ShapeDtypeStruct((m, D_MODEL), F32),
        compiler_params=pltpu.CompilerParams(
            dimension_semantics=("arbitrary",), vmem_limit_bytes=VMEM_LIMIT),
        name="out_stage",
    )(x2, ya, yr, z, z, wa, wr, wo, pw)


def _block_diag(w):
    per = MXU_DIM // RNN_BLOCK_W
    n_kb = RNN_BLOCKS // per
    w4 = w.reshape(n_kb, per, RNN_BLOCK_W, RNN_BLOCK_W)
    eye = jnp.eye(per, dtype=w.dtype)
    bd = jnp.einsum('kgij,gh->kgihj', w4, eye)
    return bd.reshape(n_kb, MXU_DIM, MXU_DIM).astype(BF16)


def kernel(x, pre_norm_w, w_in, b_in, conv_w, conv_b, rg_wa, rg_ba, rg_wx, rg_bx, rg_lambda,
           w_branch_a, w_branch_r, w_out, post_norm_w):
    bsz, seq, d = x.shape
    assert d == D_MODEL and seq % TQ == 0 and seq % T_RNN == 0
    assert (bsz * seq) % BM_IN == 0 and (bsz * seq) % BM_OUT == 0
    d_att = ATT_HEADS * HEAD_DIM
    f0 = 3 * d_att
    f1 = f0 + ATT_HEADS
    x2 = x.reshape(bsz * seq, d)
    row = lambda v: v.reshape(1, -1)
    for l in range(pre_norm_w.shape[0]):
        w = w_in[l]
        w8 = jnp.concatenate([w[:, :f0], w[:, f1:]], axis=1).astype(BF16)
        b8 = row(jnp.concatenate([b_in[l, :f0], b_in[l, f1:]]))
        wf = jnp.pad(w[:, f0:f1], ((0, 0), (0, LANES - ATT_HEADS))).astype(BF16)
        bf = row(jnp.pad(b_in[l, f0:f1], (0, LANES - ATT_HEADS)))
        z, lf = _in_proj(x2, row(pre_norm_w[l]), w8, b8, wf, bf)
        ya = _attention(z, lf, bsz, seq)
        yr = _rglru(z, conv_w[l], row(conv_b[l]), _block_diag(0.5 * rg_wa[l]),
                    row(0.5 * rg_ba[l]), _block_diag(0.5 * rg_wx[l]), row(0.5 * rg_bx[l]),
                    row(rg_lambda[l]), bsz, seq)
        x2 = _out_stage(x2, ya, yr, z, w_branch_a[l].astype(BF16), w_branch_r[l].astype(BF16),
                        w_out[l].astype(BF16), row(post_norm_w[l]))
    return x2.reshape(bsz, seq, d)
```

```python
import functools

import jax
import jax.numpy as jnp
from jax import lax
from jax.experimental import pallas as pl
from jax.experimental.pallas import tpu as pltpu

D_MODEL = 1024
ATT_HEADS = 16
HEAD_DIM = 64
RNN_BLOCKS = 16
RNN_BLOCK_W = 64
CONV_W = 4
RG_C = 8.0
NORM_EPS = 1e-6
MASK_VALUE = -1e30

LANES = 128
SUBLANES = 8
MXU_DIM = 256
N_GROUPS = 8
AUG_PER_HEAD = 6
VMEM_LIMIT = 56 * 1024 * 1024

BM_IN = 512
BM_OUT = 1024
OUT_SUBTILES = 2
T_RNN = 512
RG_STRIDE = 8
TQ = 256
ATT_PAIRS_PER_STEP = 2
VT_PAD = 16
VT_ROWS = HEAD_DIM + VT_PAD
LOG2E = 1.4426950408889634

F32 = jnp.float32
BF16 = jnp.bfloat16


def _sigmoid(x):
    return 1.0 / (1.0 + jnp.exp(-x))


def _silu(x):
    return x * _sigmoid(x)


def _log_sigmoid(x):
    return jnp.minimum(x, 0.0) - jnp.log(1.0 + jnp.exp(-jnp.abs(x)))


def _softplus(x):
    return jnp.maximum(x, 0.0) + jnp.log(1.0 + jnp.exp(-jnp.abs(x)))


def _split3(x):
    hi = x.astype(BF16)
    r1 = x - hi.astype(F32)
    mid = r1.astype(BF16)
    lo = (r1 - mid.astype(F32)).astype(BF16)
    return hi, mid, lo


def _in_proj_kernel(x_ref, nw_ref, w_ref, b_ref, wf_ref, bf_ref, z_ref, lf_ref):
    x = x_ref[...]
    var = jnp.mean(x * x, axis=-1, keepdims=True)
    h = (x * lax.rsqrt(var + NORM_EPS) * nw_ref[...]).astype(BF16)
    for g in range(N_GROUPS):
        cols = slice(g * D_MODEL, (g + 1) * D_MODEL)
        acc = jnp.dot(h, w_ref[:, cols], preferred_element_type=F32) + b_ref[:, cols]
        if g == 0:
            val = acc * (HEAD_DIM ** -0.5 * LOG2E)
        elif g in (3, 5):
            val = _silu(acc)
        elif g in (6, 7):
            val = _sigmoid(acc)
        else:
            val = acc
        z_ref[:, cols] = val.astype(BF16)
    f = jnp.dot(h, wf_ref[...], preferred_element_type=F32) + bf_ref[...]
    lf_ref[...] = _log_sigmoid(f)


def _in_proj(x2, nw, w8, b8, wf, bf):
    m = x2.shape[0]
    const = lambda i: (0, 0)
    return pl.pallas_call(
        _in_proj_kernel,
        grid=(m // BM_IN,),
        in_specs=[
            pl.BlockSpec((BM_IN, D_MODEL), lambda i: (i, 0)),
            pl.BlockSpec((1, D_MODEL), const),
            pl.BlockSpec((D_MODEL, N_GROUPS * D_MODEL), const, pipeline_mode=pl.Buffered(1)),
            pl.BlockSpec((1, N_GROUPS * D_MODEL), const),
            pl.BlockSpec((D_MODEL, LANES), const),
            pl.BlockSpec((1, LANES), const),
        ],
        out_specs=[
            pl.BlockSpec((BM_IN, N_GROUPS * D_MODEL), lambda i: (i, 0)),
            pl.BlockSpec((BM_IN, LANES), lambda i: (i, 0)),
        ],
        out_shape=[
            jax.ShapeDtypeStruct((m, N_GROUPS * D_MODEL), BF16),
            jax.ShapeDtypeStruct((m, LANES), F32),
        ],
        compiler_params=pltpu.CompilerParams(
            dimension_semantics=("arbitrary",), vmem_limit_bytes=VMEM_LIMIT),
        name="in_proj",
    )(x2, nw, w8, b8, wf, bf)


def _attn_kernel(q_ref, k_ref, v_ref, ga_ref, lf_ref, o_ref, qaug_ref, kaug_ref, vt_ref, *, seq):
    j = pl.program_id(1)
    n_q = seq // TQ

    @pl.when(j == 0)
    def _build_aug():
        row = lax.broadcasted_iota(jnp.int32, (LANES, LANES), 0)
        col = lax.broadcasted_iota(jnp.int32, (LANES, LANES), 1)
        tri = (col <= row).astype(BF16)
        lane1 = lax.broadcasted_iota(jnp.int32, (1, LANES), 1)
        n_aug = ATT_HEADS * AUG_PER_HEAD
        k_ones = ((lane1 % AUG_PER_HEAD >= 3) & (lane1 < n_aug)).astype(F32)
        q_ones = ((lane1 % AUG_PER_HEAD < 3) & (lane1 < n_aug)).astype(F32)
        sel = [((col == AUG_PER_HEAD * row + r) & (row < ATT_HEADS)).astype(BF16)
               for r in range(AUG_PER_HEAD)]
        n_blk = seq // LANES
        local = []
        for blk in range(n_blk):
            pieces = _split3(lf_ref[blk * LANES:(blk + 1) * LANES, :] * LOG2E)
            acc = jnp.dot(tri, pieces[0], preferred_element_type=F32)
            for p in pieces[1:]:
                acc = acc + jnp.dot(tri, p, preferred_element_type=F32)
            local.append(acc)
        carry = jnp.zeros((1, LANES), F32)
        for blk in range(n_blk):
            rows = slice(blk * LANES, (blk + 1) * LANES)
            c = local[blk] + carry
            carry = c[LANES - 1:LANES, :]
            c3 = _split3(c)
            kacc = k_ones
            qacc = q_ones
            for r in range(3):
                kacc = kacc - jnp.dot(c3[r], sel[r], preferred_element_type=F32)
                qacc = qacc + jnp.dot(c3[r], sel[3 + r], preferred_element_type=F32)
            kaug_ref[rows, :] = kacc.astype(BF16)
            qaug_ref[rows, :] = qacc.astype(BF16)

    one_row = (lax.broadcasted_iota(jnp.int32, (VT_PAD, seq), 0) == 0).astype(BF16)
    for lp in range(ATT_PAIRS_PER_STEP):
        v_t = v_ref[:, lp * LANES:(lp + 1) * LANES].astype(F32).T.astype(BF16)
        for e in range(2):
            vt_ref[lp, e * VT_ROWS:e * VT_ROWS + HEAD_DIM, :] = v_t[e * HEAD_DIM:(e + 1) * HEAD_DIM, :]
            vt_ref[lp, e * VT_ROWS + HEAD_DIM:(e + 1) * VT_ROWS, :] = one_row
    lane_q = lax.broadcasted_iota(jnp.int32, (TQ, LANES), 1)
    key_i = lax.broadcasted_iota(jnp.int32, (TQ, TQ), 0)
    qry_i = lax.broadcasted_iota(jnp.int32, (TQ, TQ), 1)
    nt_dims = (((1,), (1,)), ((), ()))

    def scores(lp, i, e):
        q0 = i * TQ
        head = 2 * (ATT_PAIRS_PER_STEP * j + lp) + e
        cols = slice(lp * LANES, (lp + 1) * LANES)
        q_t = q_ref[q0:q0 + TQ, cols]
        qa_t = qaug_ref[q0:q0 + TQ, :]
        qm = jnp.where(lane_q // HEAD_DIM == e, q_t, jnp.zeros_like(q_t))
        in_head = (lane_q >= AUG_PER_HEAD * head) & (lane_q < AUG_PER_HEAD * (head + 1))
        qa = jnp.where(in_head, qa_t, jnp.zeros_like(qa_t))
        qcat = jnp.concatenate([qm, qa], axis=1)
        k_diag = jnp.concatenate([k_ref[q0:q0 + TQ, cols], kaug_ref[q0:q0 + TQ, :]], axis=1)
        s_d = lax.dot_general(k_diag, qcat, nt_dims, preferred_element_type=F32)
        s_d = jnp.where(key_i <= qry_i, s_d, MASK_VALUE)
        m = jnp.max(s_d, axis=0, keepdims=True)
        s_o = None
        if i:
            k_off = jnp.concatenate([k_ref[0:q0, cols], kaug_ref[0:q0, :]], axis=1)
            s_o = lax.dot_general(k_off, qcat, nt_dims, preferred_element_type=F32)
            m = jnp.maximum(m, jnp.max(s_o, axis=0, keepdims=True))
        return s_d, s_o, m

    def weighted_values(lp, i, e, s_d, s_o, m):
        q0 = i * TQ
        rows = slice(e * VT_ROWS, (e + 1) * VT_ROWS)
        p_d = jnp.exp2((s_d - m).astype(BF16))
        o_t = jnp.dot(vt_ref[lp, rows, q0:q0 + TQ], p_d, preferred_element_type=F32)
        if i:
            p_o = jnp.exp2((s_o - m).astype(BF16))
            o_t = o_t + jnp.dot(vt_ref[lp, rows, 0:q0], p_o, preferred_element_type=F32)
        return o_t[0:HEAD_DIM, :] * (1.0 / o_t[HEAD_DIM:HEAD_DIM + 1, :])

    tiles = [(lp, i) for lp in range(ATT_PAIRS_PER_STEP) for i in range(n_q)]
    pending = None
    for tile in tiles + [None]:
        cur = [scores(*tile, e) for e in range(2)] if tile is not None else None
        if pending is not None:
            (lp, pi), stats = pending
            y_t = [weighted_values(lp, pi, e, *stats[e]) for e in range(2)]
            q0 = pi * TQ
            cols = slice(lp * LANES, (lp + 1) * LANES)
            y = jnp.concatenate(y_t, axis=0).T
            o_ref[q0:q0 + TQ, cols] = (y * ga_ref[q0:q0 + TQ, cols].astype(F32)).astype(BF16)
        pending = (tile, cur)


def _attention(z, lf, bsz, seq):
    m = bsz * seq
    n_steps = ATT_HEADS // (2 * ATT_PAIRS_PER_STEP)
    col = lambda g: (lambda b, j: (b, g * n_steps + j))
    blk = (seq, ATT_PAIRS_PER_STEP * LANES)
    aug = (seq, LANES)
    return pl.pallas_call(
        functools.partial(_attn_kernel, seq=seq),
        grid=(bsz, n_steps),
        in_specs=[
            pl.BlockSpec(blk, col(0)),
            pl.BlockSpec(blk, col(1)),
            pl.BlockSpec(blk, col(2)),
            pl.BlockSpec(blk, col(3)),
            pl.BlockSpec(aug, lambda b, j: (b, 0)),
        ],
        out_specs=pl.BlockSpec(blk, lambda b, j: (b, j)),
        out_shape=jax.ShapeDtypeStruct((m, D_MODEL), BF16),
        scratch_shapes=[pltpu.VMEM(aug, BF16), pltpu.VMEM(aug, BF16),
                        pltpu.VMEM((ATT_PAIRS_PER_STEP, 2 * VT_ROWS, seq), BF16)],
        compiler_params=pltpu.CompilerParams(
            dimension_semantics=("arbitrary", "arbitrary"), vmem_limit_bytes=VMEM_LIMIT),
        name="fox_attention",
    )(z, z, z, z, lf)


def _shift_rows(x3, d, fill):
    n_tiles, _, width = x3.shape
    fill = jnp.broadcast_to(jnp.asarray(fill, x3.dtype), (1, SUBLANES, width))
    if d % SUBLANES == 0:
        dt = d // SUBLANES
        return jnp.concatenate([jnp.broadcast_to(fill, (dt, SUBLANES, width)),
                                x3[:n_tiles - dt]], axis=0)
    sub = lax.broadcasted_iota(jnp.int32, (1, SUBLANES, 1), 1)
    prev = jnp.concatenate([fill, x3[:n_tiles - 1]], axis=0)
    return jnp.where(sub >= d, pltpu.roll(x3, d, axis=1), pltpu.roll(prev, d, axis=1))


def _rglru_kernel(xr_ref, gr_ref, perm_ref, unperm_ref, cw_ref, cb_ref, wa_ref, ba_ref, wx_ref,
                  bx_ref, lam_ref, o_ref, xs_ref, h_ref):
    t = pl.program_id(1)
    n_grp = T_RNN // RG_STRIDE
    hist = SUBLANES

    @pl.when(t == 0)
    def _reset():
        xs_ref[:, 0:hist, :] = jnp.zeros((CONV_W - 1, hist, D_MODEL), F32)
        h_ref[...] = jnp.zeros_like(h_ref)

    blk = lambda v, r: v[r * n_grp:(r + 1) * n_grp]
    xp = jnp.dot(perm_ref[...], xr_ref[...], preferred_element_type=F32)
    grp = jnp.dot(perm_ref[...], gr_ref[...], preferred_element_type=F32)
    first = RG_STRIDE - CONV_W + 1
    x_prev = {}
    for r in range(first, RG_STRIDE):
        xs_ref[r - first, hist:hist + n_grp, :] = blk(xp, r)
        x_prev[r] = xs_ref[r - first, pl.ds(hist - 1, n_grp), :]
        xs_ref[r - first, 0:hist, :] = xs_ref[r - first, n_grp:n_grp + hist, :]
    src = lambda r: blk(xp, r) if r >= 0 else x_prev[r + RG_STRIDE]
    xc_blocks = []
    for r in range(RG_STRIDE):
        acc = cb_ref[...] + cw_ref[0:1, :] * src(r - CONV_W + 1)
        for tap in range(1, CONV_W):
            acc = acc + cw_ref[tap:tap + 1, :] * src(r - CONV_W + 1 + tap)
        xc_blocks.append(acc)
    xc = jnp.concatenate(xc_blocks, axis=0)

    xcb = xc.astype(BF16)
    half_c_sp = (0.5 * RG_C) * _softplus(-lam_ref[...])
    tiles = n_grp // SUBLANES
    for kb in range(D_MODEL // MXU_DIM):
        cols = slice(kb * MXU_DIM, (kb + 1) * MXU_DIM)
        xk = xcb[:, cols]
        t_r = jnp.tanh(jnp.dot(xk, wa_ref[kb], preferred_element_type=F32) + ba_ref[:, cols])
        t_i = jnp.tanh(jnp.dot(xk, wx_ref[kb], preferred_element_type=F32) + bx_ref[:, cols])
        neg_log_a = half_c_sp[:, cols] + half_c_sp[:, cols] * t_r
        a = jnp.exp(-neg_log_a)
        one_m_a2 = (1.0 + a * a) * jnp.tanh(neg_log_a)
        mult = jnp.where(one_m_a2 > 0.0, one_m_a2 * lax.rsqrt(one_m_a2), 0.0)
        u = mult * ((0.5 + 0.5 * t_i) * xc[:, cols])

        a_in = [blk(a, 0)]
        h_in = [blk(u, 0)]
        for r in range(1, RG_STRIDE):
            h_in.append(blk(a, r) * h_in[-1] + blk(u, r))
            a_in.append(blk(a, r) * a_in[-1])
        a_g = a_in[-1].reshape(tiles, SUBLANES, MXU_DIM)
        h_g = h_in[-1].reshape(tiles, SUBLANES, MXU_DIM)
        d = 1
        while d < n_grp:
            h_g = h_g + a_g * _shift_rows(h_g, d, 0.0)
            a_g = a_g * _shift_rows(a_g, d, 1.0)
            d *= 2
        h0 = h_ref[:, cols]
        h_end = h_g + a_g * h0
        h_ent = _shift_rows(h_end, 1, h0).reshape(n_grp, MXU_DIM)
        h_ref[:, cols] = h_end[tiles - 1, SUBLANES - 1:SUBLANES, :]
        y = jnp.concatenate(
            [(h_in[r] + a_in[r] * h_ent) * blk(grp, r)[:, cols] for r in range(RG_STRIDE)], axis=0)
        o_ref[:, cols] = jnp.dot(unperm_ref[...], y.astype(BF16),
                                 preferred_element_type=F32).astype(BF16)


def _rglru(z, cw, cb, wa_bd, ba, wx_bd, bx, lam, bsz, seq):
    m = bsz * seq
    n_t = seq // T_RNN
    const2 = lambda b, t: (0, 0)
    const3 = lambda b, t: (0, 0, 0)
    n_kb = D_MODEL // MXU_DIM
    n_grp = T_RNN // RG_STRIDE
    pos = jnp.arange(T_RNN)
    step_of_row = (pos % n_grp) * RG_STRIDE + pos // n_grp
    perm = (step_of_row[:, None] == pos[None, :]).astype(BF16)
    return pl.pallas_call(
        _rglru_kernel,
        grid=(bsz, n_t),
        in_specs=[
            pl.BlockSpec((T_RNN, D_MODEL), lambda b, t: (b * n_t + t, 4)),
            pl.BlockSpec((T_RNN, D_MODEL), lambda b, t: (b * n_t + t, 5)),
            pl.BlockSpec((T_RNN, T_RNN), const2),
            pl.BlockSpec((T_RNN, T_RNN), const2),
            pl.BlockSpec((CONV_W, D_MODEL), const2),
            pl.BlockSpec((1, D_MODEL), const2),
            pl.BlockSpec((n_kb, MXU_DIM, MXU_DIM), const3),
            pl.BlockSpec((1, D_MODEL), const2),
            pl.BlockSpec((n_kb, MXU_DIM, MXU_DIM), const3),
            pl.BlockSpec((1, D_MODEL), const2),
            pl.BlockSpec((1, D_MODEL), const2),
        ],
        out_specs=pl.BlockSpec((T_RNN, D_MODEL), lambda b, t: (b * n_t + t, 0)),
        out_shape=jax.ShapeDtypeStruct((m, D_MODEL), BF16),
        scratch_shapes=[
            pltpu.VMEM((CONV_W - 1, SUBLANES + n_grp, D_MODEL), F32),
            pltpu.VMEM((1, D_MODEL), F32),
        ],
        compiler_params=pltpu.CompilerParams(
            dimension_semantics=("arbitrary", "arbitrary"), vmem_limit_bytes=VMEM_LIMIT),
        name="rglru",
    )(z, z, perm, perm.T, cw, cb, wa_bd, ba, wx_bd, bx, lam)


def _out_kernel(x_ref, ya_ref, yr_ref, sa_ref, sr_ref, wa_ref, wr_ref, wo_ref, pw_ref, o_ref):
    sub_rows = BM_OUT // OUT_SUBTILES
    for part in range(OUT_SUBTILES):
        rows = slice(part * sub_rows, (part + 1) * sub_rows)
        pa = jnp.dot(ya_ref[rows, :], wa_ref[...], preferred_element_type=F32)
        pr = jnp.dot(yr_ref[rows, :], wr_ref[...], preferred_element_type=F32)
        mrg = sa_ref[rows, :].astype(F32) * pa + sr_ref[rows, :].astype(F32) * pr
        o = jnp.dot(mrg.astype(BF16), wo_ref[...], preferred_element_type=F32)
        var = jnp.mean(o * o, axis=-1, keepdims=True)
        o_ref[rows, :] = x_ref[rows, :] + o * lax.rsqrt(var + NORM_EPS) * pw_ref[...]


def _out_stage(x2, ya, yr, z, wa, wr, wo, pw):
    m = x2.shape[0]
    const = lambda i: (0, 0)
    row = lambda i: (i, 0)
    blk = (BM_OUT, D_MODEL)
    wblk = (D_MODEL, D_MODEL)
    return pl.pallas_call(
        _out_kernel,
        grid=(m // BM_OUT,),
        in_specs=[
            pl.BlockSpec(blk, row),
            pl.BlockSpec(blk, row),
            pl.BlockSpec(blk, row),
            pl.BlockSpec(blk, lambda i: (i, 6)),
            pl.BlockSpec(blk, lambda i: (i, 7)),
            pl.BlockSpec(wblk, const),
            pl.BlockSpec(wblk, const),
            pl.BlockSpec(wblk, const),
            pl.BlockSpec((1, D_MODEL), const),
        ],
        out_specs=pl.BlockSpec(blk, row),
        out_shape=jax.ShapeDtypeStruct((m, D_MODEL), F32),
        compiler_params=pltpu.CompilerParams(
            dimension_semantics=("arbitrary",), vmem_limit_bytes=VMEM_LIMIT),
        name="out_stage",
    )(x2, ya, yr, z, z, wa, wr, wo, pw)


def _block_diag(w):
    per = MXU_DIM // RNN_BLOCK_W
    n_kb = RNN_BLOCKS // per
    w4 = w.reshape(n_kb, per, RNN_BLOCK_W, RNN_BLOCK_W)
    eye = jnp.eye(per, dtype=w.dtype)
    bd = jnp.einsum('kgij,gh->kgihj', w4, eye)
    return bd.reshape(n_kb, MXU_DIM, MXU_DIM).astype(BF16)


def kernel(x, pre_norm_w, w_in, b_in, conv_w, conv_b, rg_wa, rg_ba, rg_wx, rg_bx, rg_lambda,
           w_branch_a, w_branch_r, w_out, post_norm_w):
    bsz, seq, d = x.shape
    assert d == D_MODEL and seq % TQ == 0 and seq % T_RNN == 0
    assert (bsz * seq) % BM_IN == 0 and (bsz * seq) % BM_OUT == 0
    d_att = ATT_HEADS * HEAD_DIM
    f0 = 3 * d_att
    f1 = f0 + ATT_HEADS
    x2 = x.reshape(bsz * seq, d)
    row = lambda v: v.reshape(1, -1)
    for l in range(pre_norm_w.shape[0]):
        w = w_in[l]
        w8 = jnp.concatenate([w[:, :f0], w[:, f1:]], axis=1).astype(BF16)
        b8 = row(jnp.concatenate([b_in[l, :f0], b_in[l, f1:]]))
        wf = jnp.pad(w[:, f0:f1], ((0, 0), (0, LANES - ATT_HEADS))).astype(BF16)
        bf = row(jnp.pad(b_in[l, f0:f1], (0, LANES - ATT_HEADS)))
        z, lf = _in_proj(x2, row(pre_norm_w[l]), w8, b8, wf, bf)
        ya = _attention(z, lf, bsz, seq)
        yr = _rglru(z, conv_w[l], row(conv_b[l]), _block_diag(0.5 * rg_wa[l]),
                    row(0.5 * rg_ba[l]), _block_diag(0.5 * rg_wx[l]), row(0.5 * rg_bx[l]),
                    row(rg_lambda[l]), bsz, seq)
        x2 = _out_stage(x2, ya, yr, z, w_branch_a[l].astype(BF16), w_branch_r[l].astype(BF16),
                        w_out[l].astype(BF16), row(post_norm_w[l]))
    return x2.reshape(bsz, seq, d)
```

```python
import functools

import jax
import jax.numpy as jnp
from jax import lax
from jax.experimental import pallas as pl
from jax.experimental.pallas import tpu as pltpu

D_MODEL = 1024
ATT_HEADS = 16
HEAD_DIM = 64
RNN_BLOCKS = 16
RNN_BLOCK_W = 64
CONV_W = 4
RG_C = 8.0
NORM_EPS = 1e-6
MASK_VALUE = -1e30

LANES = 128
SUBLANES = 8
MXU_DIM = 256
N_GROUPS = 8
AUG_PER_HEAD = 6
VMEM_LIMIT = 56 * 1024 * 1024

BM_IN = 512
BM_OUT = 1024
OUT_SUBTILES = 2
T_RNN = 512
RG_STRIDE = 8
TQ = 256
ATT_PAIRS_PER_STEP = 4
VT_PAD = 16
VT_ROWS = HEAD_DIM + VT_PAD
LOG2E = 1.4426950408889634

F32 = jnp.float32
BF16 = jnp.bfloat16


def _sigmoid(x):
    return 0.5 + 0.5 * jnp.tanh(0.5 * x)


def _log_sigmoid(x):
    return jnp.minimum(x, 0.0) - jnp.log(1.0 + jnp.exp(-jnp.abs(x)))


def _softplus(x):
    return jnp.maximum(x, 0.0) + jnp.log(1.0 + jnp.exp(-jnp.abs(x)))


def _split3(x):
    hi = x.astype(BF16)
    r1 = x - hi.astype(F32)
    mid = r1.astype(BF16)
    lo = (r1 - mid.astype(F32)).astype(BF16)
    return hi, mid, lo


def _in_proj_kernel(x_ref, nw_ref, w_ref, b_ref, wf_ref, bf_ref, z_ref, lf_ref):
    x = x_ref[...]
    var = jnp.mean(x * x, axis=-1, keepdims=True)
    h = (x * lax.rsqrt(var + NORM_EPS) * nw_ref[...]).astype(BF16)
    f = jnp.dot(h, wf_ref[...], preferred_element_type=F32) + bf_ref[...]
    lf_ref[...] = _log_sigmoid(f)
    for g in (3, 5, 6, 7, 0, 1, 2, 4):
        cols = slice(g * D_MODEL, (g + 1) * D_MODEL)
        acc = jnp.dot(h, w_ref[:, cols], preferred_element_type=F32) + b_ref[:, cols]
        if g == 0:
            val = acc * (HEAD_DIM ** -0.5 * LOG2E)
        elif g in (3, 5):
            val = acc * _sigmoid(acc)
        elif g in (6, 7):
            val = _sigmoid(acc)
        else:
            val = acc
        z_ref[:, cols] = val.astype(BF16)


def _in_proj(x2, nw, w8, b8, wf, bf):
    m = x2.shape[0]
    const = lambda i: (0, 0)
    return pl.pallas_call(
        _in_proj_kernel,
        grid=(m // BM_IN,),
        in_specs=[
            pl.BlockSpec((BM_IN, D_MODEL), lambda i: (i, 0)),
            pl.BlockSpec((1, D_MODEL), const),
            pl.BlockSpec((D_MODEL, N_GROUPS * D_MODEL), const, pipeline_mode=pl.Buffered(1)),
            pl.BlockSpec((1, N_GROUPS * D_MODEL), const),
            pl.BlockSpec((D_MODEL, LANES), const),
            pl.BlockSpec((1, LANES), const),
        ],
        out_specs=[
            pl.BlockSpec((BM_IN, N_GROUPS * D_MODEL), lambda i: (i, 0)),
            pl.BlockSpec((BM_IN, LANES), lambda i: (i, 0)),
        ],
        out_shape=[
            jax.ShapeDtypeStruct((m, N_GROUPS * D_MODEL), BF16),
            jax.ShapeDtypeStruct((m, LANES), F32),
        ],
        compiler_params=pltpu.CompilerParams(
            dimension_semantics=("arbitrary",), vmem_limit_bytes=VMEM_LIMIT),
        name="in_proj",
    )(x2, nw, w8, b8, wf, bf)


def _attn_kernel(q_ref, k_ref, v_ref, ga_ref, lf_ref, o_ref, qaug_ref, kaug_ref, vt_ref, *, seq):
    j = pl.program_id(1)
    n_q = seq // TQ

    @pl.when(j == 0)
    def _build_aug():
        row = lax.broadcasted_iota(jnp.int32, (LANES, LANES), 0)
        col = lax.broadcasted_iota(jnp.int32, (LANES, LANES), 1)
        tri = (col <= row).astype(BF16)
        lane1 = lax.broadcasted_iota(jnp.int32, (1, LANES), 1)
        n_aug = ATT_HEADS * AUG_PER_HEAD
        k_ones = ((lane1 % AUG_PER_HEAD >= 3) & (lane1 < n_aug)).astype(F32)
        q_ones = ((lane1 % AUG_PER_HEAD < 3) & (lane1 < n_aug)).astype(F32)
        sel = [((col == AUG_PER_HEAD * row + r) & (row < ATT_HEADS)).astype(BF16)
               for r in range(AUG_PER_HEAD)]
        n_blk = seq // LANES
        local = []
        for blk in range(n_blk):
            pieces = _split3(lf_ref[blk * LANES:(blk + 1) * LANES, :] * LOG2E)
            acc = jnp.dot(tri, pieces[0], preferred_element_type=F32)
            for p in pieces[1:]:
                acc = acc + jnp.dot(tri, p, preferred_element_type=F32)
            local.append(acc)
        carry = jnp.zeros((1, LANES), F32)
        for blk in range(n_blk):
            rows = slice(blk * LANES, (blk + 1) * LANES)
            c = local[blk] + carry
            carry = c[LANES - 1:LANES, :]
            c3 = _split3(c)
            kacc = k_ones
            qacc = q_ones
            for r in range(3):
                kacc = kacc - jnp.dot(c3[r], sel[r], preferred_element_type=F32)
                qacc = qacc + jnp.dot(c3[r], sel[3 + r], preferred_element_type=F32)
            kaug_ref[rows, :] = kacc.astype(BF16)
            qaug_ref[rows, :] = qacc.astype(BF16)

    one_row = (lax.broadcasted_iota(jnp.int32, (VT_PAD, seq), 0) == 0).astype(BF16)
    for lp in range(ATT_PAIRS_PER_STEP):
        v_t = v_ref[:, lp * LANES:(lp + 1) * LANES].astype(F32).T.astype(BF16)
        for e in range(2):
            vt_ref[lp, e * VT_ROWS:e * VT_ROWS + HEAD_DIM, :] = v_t[e * HEAD_DIM:(e + 1) * HEAD_DIM, :]
            vt_ref[lp, e * VT_ROWS + HEAD_DIM:(e + 1) * VT_ROWS, :] = one_row
    lane_q = lax.broadcasted_iota(jnp.int32, (TQ, LANES), 1)
    key_i = lax.broadcasted_iota(jnp.int32, (TQ, TQ), 0)
    qry_i = lax.broadcasted_iota(jnp.int32, (TQ, TQ), 1)
    nt_dims = (((1,), (1,)), ((), ()))

    def scores(lp, i, e):
        q0 = i * TQ
        head = 2 * (ATT_PAIRS_PER_STEP * j + lp) + e
        cols = slice(lp * LANES, (lp + 1) * LANES)
        q_t = q_ref[q0:q0 + TQ, cols]
        qa_t = qaug_ref[q0:q0 + TQ, :]
        qm = jnp.where(lane_q // HEAD_DIM == e, q_t, jnp.zeros_like(q_t))
        in_head = (lane_q >= AUG_PER_HEAD * head) & (lane_q < AUG_PER_HEAD * (head + 1))
        qa = jnp.where(in_head, qa_t, jnp.zeros_like(qa_t))
        qcat = jnp.concatenate([qm, qa], axis=1)
        k_diag = jnp.concatenate([k_ref[q0:q0 + TQ, cols], kaug_ref[q0:q0 + TQ, :]], axis=1)
        s_d = lax.dot_general(k_diag, qcat, nt_dims, preferred_element_type=F32)
        s_d = jnp.where(key_i <= qry_i, s_d, MASK_VALUE)
        m = jnp.max(s_d, axis=0, keepdims=True)
        s_o = None
        if i:
            k_off = jnp.concatenate([k_ref[0:q0, cols], kaug_ref[0:q0, :]], axis=1)
            s_o = lax.dot_general(k_off, qcat, nt_dims, preferred_element_type=F32)
            m = jnp.maximum(m, jnp.max(s_o, axis=0, keepdims=True))
        return s_d, s_o, m

    def weighted_values(lp, i, e, s_d, s_o, m):
        q0 = i * TQ
        rows = slice(e * VT_ROWS, (e + 1) * VT_ROWS)
        p_d = jnp.exp2((s_d - m).astype(BF16))
        o_t = jnp.dot(vt_ref[lp, rows, q0:q0 + TQ], p_d, preferred_element_type=F32)
        if i:
            p_o = jnp.exp2((s_o - m).astype(BF16))
            o_t = o_t + jnp.dot(vt_ref[lp, rows, 0:q0], p_o, preferred_element_type=F32)
        return o_t[0:HEAD_DIM, :] * (1.0 / o_t[HEAD_DIM:HEAD_DIM + 1, :])

    tiles = [(lp, i) for lp in range(ATT_PAIRS_PER_STEP) for i in range(n_q)]
    pending = None
    for tile in tiles + [None]:
        cur = [scores(*tile, e) for e in range(2)] if tile is not None else None
        if pending is not None:
            (lp, pi), stats = pending
            y_t = [weighted_values(lp, pi, e, *stats[e]) for e in range(2)]
            q0 = pi * TQ
            cols = slice(lp * LANES, (lp + 1) * LANES)
            y = jnp.concatenate(y_t, axis=0).T
            o_ref[q0:q0 + TQ, cols] = (y * ga_ref[q0:q0 + TQ, cols].astype(F32)).astype(BF16)
        pending = (tile, cur)


def _attention(z, lf, bsz, seq):
    m = bsz * seq
    n_steps = ATT_HEADS // (2 * ATT_PAIRS_PER_STEP)
    col = lambda g: (lambda b, j: (b, g * n_steps + j))
    blk = (seq, ATT_PAIRS_PER_STEP * LANES)
    aug = (seq, LANES)
    return pl.pallas_call(
        functools.partial(_attn_kernel, seq=seq),
        grid=(bsz, n_steps),
        in_specs=[
            pl.BlockSpec(blk, col(0)),
            pl.BlockSpec(blk, col(1)),
            pl.BlockSpec(blk, col(2)),
            pl.BlockSpec(blk, col(3)),
            pl.BlockSpec(aug, lambda b, j: (b, 0)),
        ],
        out_specs=pl.BlockSpec(blk, lambda b, j: (b, j)),
        out_shape=jax.ShapeDtypeStruct((m, D_MODEL), BF16),
        scratch_shapes=[pltpu.VMEM(aug, BF16), pltpu.VMEM(aug, BF16),
                        pltpu.VMEM((ATT_PAIRS_PER_STEP, 2 * VT_ROWS, seq), BF16)],
        compiler_params=pltpu.CompilerParams(
            dimension_semantics=("arbitrary", "arbitrary"), vmem_limit_bytes=VMEM_LIMIT),
        name="fox_attention",
    )(z, z, z, z, lf)


def _shift_rows(x3, d, fill):
    n_tiles, _, width = x3.shape
    fill = jnp.broadcast_to(jnp.asarray(fill, x3.dtype), (1, SUBLANES, width))
    if d % SUBLANES == 0:
        dt = d // SUBLANES
        return jnp.concatenate([jnp.broadcast_to(fill, (dt, SUBLANES, width)),
                                x3[:n_tiles - dt]], axis=0)
    sub = lax.broadcasted_iota(jnp.int32, (1, SUBLANES, 1), 1)
    prev = jnp.concatenate([fill, x3[:n_tiles - 1]], axis=0)
    return jnp.where(sub >= d, pltpu.roll(x3, d, axis=1), pltpu.roll(prev, d, axis=1))


def _rglru_kernel(xr_ref, gr_ref, perm_ref, unperm_ref, cw_ref, cb_ref, wa_ref, ba_ref, wx_ref,
                  bx_ref, lam_ref, o_ref, xs_ref, h_ref):
    t = pl.program_id(1)
    n_grp = T_RNN // RG_STRIDE
    hist = SUBLANES

    @pl.when(t == 0)
    def _reset():
        xs_ref[:, 0:hist, :] = jnp.zeros((CONV_W - 1, hist, D_MODEL), F32)
        h_ref[...] = jnp.zeros_like(h_ref)

    blk = lambda v, r: v[r * n_grp:(r + 1) * n_grp]
    xp = jnp.dot(perm_ref[...], xr_ref[...], preferred_element_type=F32)
    grp = jnp.dot(perm_ref[...], gr_ref[...], preferred_element_type=F32)
    first = RG_STRIDE - CONV_W + 1
    x_prev = {}
    for r in range(first, RG_STRIDE):
        xs_ref[r - first, hist:hist + n_grp, :] = blk(xp, r)
        x_prev[r] = xs_ref[r - first, pl.ds(hist - 1, n_grp), :]
        xs_ref[r - first, 0:hist, :] = xs_ref[r - first, n_grp:n_grp + hist, :]
    src = lambda r: blk(xp, r) if r >= 0 else x_prev[r + RG_STRIDE]
    xc_blocks = []
    for r in range(RG_STRIDE):
        acc = cb_ref[...] + cw_ref[0:1, :] * src(r - CONV_W + 1)
        for tap in range(1, CONV_W):
            acc = acc + cw_ref[tap:tap + 1, :] * src(r - CONV_W + 1 + tap)
        xc_blocks.append(acc)
    xc = jnp.concatenate(xc_blocks, axis=0)

    xcb = xc.astype(BF16)
    half_c_sp = (0.5 * RG_C) * _softplus(-lam_ref[...])
    tiles = n_grp // SUBLANES
    for kb in range(D_MODEL // MXU_DIM):
        cols = slice(kb * MXU_DIM, (kb + 1) * MXU_DIM)
        xk = xcb[:, cols]
        t_r = jnp.tanh(jnp.dot(xk, wa_ref[kb], preferred_element_type=F32) + ba_ref[:, cols])
        t_i = jnp.tanh(jnp.dot(xk, wx_ref[kb], preferred_element_type=F32) + bx_ref[:, cols])
        neg_log_a = half_c_sp[:, cols] + half_c_sp[:, cols] * t_r
        a = jnp.exp(-neg_log_a)
        one_m_a2 = (1.0 + a * a) * jnp.tanh(neg_log_a)
        mult = jnp.where(one_m_a2 > 0.0, one_m_a2 * lax.rsqrt(one_m_a2), 0.0)
        u = mult * ((0.5 + 0.5 * t_i) * xc[:, cols])

        a_in = [blk(a, 0)]
        h_in = [blk(u, 0)]
        for r in range(1, RG_STRIDE):
            h_in.append(blk(a, r) * h_in[-1] + blk(u, r))
            a_in.append(blk(a, r) * a_in[-1])
        a_g = a_in[-1].reshape(tiles, SUBLANES, MXU_DIM)
        h_g = h_in[-1].reshape(tiles, SUBLANES, MXU_DIM)
        d = 1
        while d < n_grp:
            h_g = h_g + a_g * _shift_rows(h_g, d, 0.0)
            a_g = a_g * _shift_rows(a_g, d, 1.0)
            d *= 2
        h0 = h_ref[:, cols]
        h_end = h_g + a_g * h0
        h_ent = _shift_rows(h_end, 1, h0).reshape(n_grp, MXU_DIM)
        h_ref[:, cols] = h_end[tiles - 1, SUBLANES - 1:SUBLANES, :]
        y = jnp.concatenate(
            [(h_in[r] + a_in[r] * h_ent) * blk(grp, r)[:, cols] for r in range(RG_STRIDE)], axis=0)
        o_ref[:, cols] = jnp.dot(unperm_ref[...], y.astype(BF16),
                                 preferred_element_type=F32).astype(BF16)


def _rglru(z, cw, cb, wa_bd, ba, wx_bd, bx, lam, bsz, seq):
    m = bsz * seq
    n_t = seq // T_RNN
    const2 = lambda b, t: (0, 0)
    const3 = lambda b, t: (0, 0, 0)
    n_kb = D_MODEL // MXU_DIM
    n_grp = T_RNN // RG_STRIDE
    pos = jnp.arange(T_RNN)
    step_of_row = (pos % n_grp) * RG_STRIDE + pos // n_grp
    perm = (step_of_row[:, None] == pos[None, :]).astype(BF16)
    return pl.pallas_call(
        _rglru_kernel,
        grid=(bsz, n_t),
        in_specs=[
            pl.BlockSpec((T_RNN, D_MODEL), lambda b, t: (b * n_t + t, 4)),
            pl.BlockSpec((T_RNN, D_MODEL), lambda b, t: (b * n_t + t, 5)),
            pl.BlockSpec((T_RNN, T_RNN), const2),
            pl.BlockSpec((T_RNN, T_RNN), const2),
            pl.BlockSpec((CONV_W, D_MODEL), const2),
            pl.BlockSpec((1, D_MODEL), const2),
            pl.BlockSpec((n_kb, MXU_DIM, MXU_DIM), const3),
            pl.BlockSpec((1, D_MODEL), const2),
            pl.BlockSpec((n_kb, MXU_DIM, MXU_DIM), const3),
            pl.BlockSpec((1, D_MODEL), const2),
            pl.BlockSpec((1, D_MODEL), const2),
        ],
        out_specs=pl.BlockSpec((T_RNN, D_MODEL), lambda b, t: (b * n_t + t, 0)),
        out_shape=jax.ShapeDtypeStruct((m, D_MODEL), BF16),
        scratch_shapes=[
            pltpu.VMEM((CONV_W - 1, SUBLANES + n_grp, D_MODEL), F32),
            pltpu.VMEM((1, D_MODEL), F32),
        ],
        compiler_params=pltpu.CompilerParams(
            dimension_semantics=("arbitrary", "arbitrary"), vmem_limit_bytes=VMEM_LIMIT),
        name="rglru",
    )(z, z, perm, perm.T, cw, cb, wa_bd, ba, wx_bd, bx, lam)


def _out_kernel(x_ref, ya_ref, yr_ref, sa_ref, sr_ref, wa_ref, wr_ref, wo_ref, pw_ref, o_ref):
    sub_rows = BM_OUT // OUT_SUBTILES
    for part in range(OUT_SUBTILES):
        rows = slice(part * sub_rows, (part + 1) * sub_rows)
        pa = jnp.dot(ya_ref[rows, :], wa_ref[...], preferred_element_type=F32)
        pr = jnp.dot(yr_ref[rows, :], wr_ref[...], preferred_element_type=F32)
        mrg = sa_ref[rows, :].astype(F32) * pa + sr_ref[rows, :].astype(F32) * pr
        o = jnp.dot(mrg.astype(BF16), wo_ref[...], preferred_element_type=F32)
        var = jnp.mean(o * o, axis=-1, keepdims=True)
        o_ref[rows, :] = x_ref[rows, :] + o * lax.rsqrt(var + NORM_EPS) * pw_ref[...]


def _out_stage(x2, ya, yr, z, wa, wr, wo, pw):
    m = x2.shape[0]
    const = lambda i: (0, 0)
    row = lambda i: (i, 0)
    blk = (BM_OUT, D_MODEL)
    wblk = (D_MODEL, D_MODEL)
    return pl.pallas_call(
        _out_kernel,
        grid=(m // BM_OUT,),
        in_specs=[
            pl.BlockSpec(blk, row),
            pl.BlockSpec(blk, row),
            pl.BlockSpec(blk, row),
            pl.BlockSpec(blk, lambda i: (i, 6)),
            pl.BlockSpec(blk, lambda i: (i, 7)),
            pl.BlockSpec(wblk, const),
            pl.BlockSpec(wblk, const),
            pl.BlockSpec(wblk, const),
            pl.BlockSpec((1, D_MODEL), const),
        ],
        out_specs=pl.BlockSpec(blk, row),
        out_shape=jax.ShapeDtypeStruct((m, D_MODEL), F32),
        compiler_params=pltpu.CompilerParams(
            dimension_semantics=("arbitrary",), vmem_limit_bytes=VMEM_LIMIT),
        name="out_stage",
    )(x2, ya, yr, z, z, wa, wr, wo, pw)


def _block_diag(w):
    per = MXU_DIM // RNN_BLOCK_W
    n_kb = RNN_BLOCKS // per
    w4 = w.reshape(n_kb, per, RNN_BLOCK_W, RNN_BLOCK_W)
    eye = jnp.eye(per, dtype=w.dtype)
    bd = jnp.einsum('kgij,gh->kgihj', w4, eye)
    return bd.reshape(n_kb, MXU_DIM, MXU_DIM).astype(BF16)


def kernel(x, pre_norm_w, w_in, b_in, conv_w, conv_b, rg_wa, rg_ba, rg_wx, rg_bx, rg_lambda,
           w_branch_a, w_branch_r, w_out, post_norm_w):
    bsz, seq, d = x.shape
    assert d == D_MODEL and seq % TQ == 0 and seq % T_RNN == 0
    assert (bsz * seq) % BM_IN == 0 and (bsz * seq) % BM_OUT == 0
    d_att = ATT_HEADS * HEAD_DIM
    f0 = 3 * d_att
    f1 = f0 + ATT_HEADS
    x2 = x.reshape(bsz * seq, d)
    row = lambda v: v.reshape(1, -1)
    for l in range(pre_norm_w.shape[0]):
        w = w_in[l]
        w8 = jnp.concatenate([w[:, :f0], w[:, f1:]], axis=1).astype(BF16)
        b8 = row(jnp.concatenate([b_in[l, :f0], b_in[l, f1:]]))
        wf = jnp.pad(w[:, f0:f1], ((0, 0), (0, LANES - ATT_HEADS))).astype(BF16)
        bf = row(jnp.pad(b_in[l, f0:f1], (0, LANES - ATT_HEADS)))
        z, lf = _in_proj(x2, row(pre_norm_w[l]), w8, b8, wf, bf)
        ya = _attention(z, lf, bsz, seq)
        yr = _rglru(z, conv_w[l], row(conv_b[l]), _block_diag(0.5 * rg_wa[l]),
                    row(0.5 * rg_ba[l]), _block_diag(0.5 * rg_wx[l]), row(0.5 * rg_bx[l]),
                    row(rg_lambda[l]), bsz, seq)
        x2 = _out_stage(x2, ya, yr, z, w_branch_a[l].astype(BF16), w_branch_r[l].astype(BF16),
                        w_out[l].astype(BF16), row(post_norm_w[l]))
    return x2.reshape(bsz, seq, d)
```

```python
import functools

import jax
import jax.numpy as jnp
from jax import lax
from jax.experimental import pallas as pl
from jax.experimental.pallas import tpu as pltpu

D_MODEL = 1024
ATT_HEADS = 16
HEAD_DIM = 64
RNN_BLOCKS = 16
RNN_BLOCK_W = 64
CONV_W = 4
RG_C = 8.0
NORM_EPS = 1e-6
MASK_VALUE = -1e30

LANES = 128
SUBLANES = 8
MXU_DIM = 256
N_GROUPS = 8
AUG_PER_HEAD = 6
VMEM_LIMIT = 56 * 1024 * 1024

BM_IN = 512
BM_OUT = 1024
OUT_SUBTILES = 2
T_RNN = 512
RG_STRIDE = 8
TQ = 256
ATT_PAIRS_PER_STEP = 4
VT_PAD = 16
VT_ROWS = HEAD_DIM + VT_PAD
LOG2E = 1.4426950408889634

F32 = jnp.float32
BF16 = jnp.bfloat16


def _sigmoid(x):
    return 0.5 + 0.5 * jnp.tanh(0.5 * x)


def _log_sigmoid(x):
    return jnp.minimum(x, 0.0) - jnp.log(1.0 + jnp.exp(-jnp.abs(x)))


def _softplus(x):
    return jnp.maximum(x, 0.0) + jnp.log(1.0 + jnp.exp(-jnp.abs(x)))


def _split3(x):
    hi = x.astype(BF16)
    r1 = x - hi.astype(F32)
    mid = r1.astype(BF16)
    lo = (r1 - mid.astype(F32)).astype(BF16)
    return hi, mid, lo


def _in_proj_kernel(x_ref, xn_ref, nw_ref, w_ref, b_ref, wf_ref, bf_ref, z_ref, lf_ref, h_ref):
    i = pl.program_id(0)

    def normed(ref):
        x = ref[...]
        var = jnp.mean(x * x, axis=-1, keepdims=True)
        return (x * lax.rsqrt(var + NORM_EPS) * nw_ref[...]).astype(BF16)

    @pl.when(i == 0)
    def _first():
        h_ref[0] = normed(x_ref)

    slot = i % 2
    h = h_ref[slot]
    f = jnp.dot(h, wf_ref[...], preferred_element_type=F32) + bf_ref[...]
    lf_ref[...] = _log_sigmoid(f)
    for g in (3, 5, 6, 7, 0, 1, 2, 4):
        if g == 5:
            h_ref[1 - slot] = normed(xn_ref)
        cols = slice(g * D_MODEL, (g + 1) * D_MODEL)
        acc = jnp.dot(h, w_ref[:, cols], preferred_element_type=F32) + b_ref[:, cols]
        if g == 0:
            val = acc * (HEAD_DIM ** -0.5 * LOG2E)
        elif g in (3, 5):
            val = acc * _sigmoid(acc)
        elif g in (6, 7):
            val = _sigmoid(acc)
        else:
            val = acc
        z_ref[:, cols] = val.astype(BF16)


def _in_proj(x2, nw, w8, b8, wf, bf):
    m = x2.shape[0]
    const = lambda i: (0, 0)
    last = m // BM_IN - 1
    return pl.pallas_call(
        _in_proj_kernel,
        grid=(m // BM_IN,),
        in_specs=[
            pl.BlockSpec((BM_IN, D_MODEL), lambda i: (i, 0)),
            pl.BlockSpec((BM_IN, D_MODEL), lambda i: (jnp.minimum(i + 1, last), 0)),
            pl.BlockSpec((1, D_MODEL), const),
            pl.BlockSpec((D_MODEL, N_GROUPS * D_MODEL), const, pipeline_mode=pl.Buffered(1)),
            pl.BlockSpec((1, N_GROUPS * D_MODEL), const),
            pl.BlockSpec((D_MODEL, LANES), const),
            pl.BlockSpec((1, LANES), const),
        ],
        out_specs=[
            pl.BlockSpec((BM_IN, N_GROUPS * D_MODEL), lambda i: (i, 0)),
            pl.BlockSpec((BM_IN, LANES), lambda i: (i, 0)),
        ],
        out_shape=[
            jax.ShapeDtypeStruct((m, N_GROUPS * D_MODEL), BF16),
            jax.ShapeDtypeStruct((m, LANES), F32),
        ],
        scratch_shapes=[pltpu.VMEM((2, BM_IN, D_MODEL), BF16)],
        compiler_params=pltpu.CompilerParams(
            dimension_semantics=("arbitrary",), vmem_limit_bytes=VMEM_LIMIT),
        name="in_proj",
    )(x2, x2, nw, w8, b8, wf, bf)


def _attn_kernel(q_ref, k_ref, v_ref, ga_ref, lf_ref, o_ref, qaug_ref, kaug_ref, vt_ref, *, seq):
    j = pl.program_id(1)
    n_q = seq // TQ

    @pl.when(j == 0)
    def _build_aug():
        row = lax.broadcasted_iota(jnp.int32, (LANES, LANES), 0)
        col = lax.broadcasted_iota(jnp.int32, (LANES, LANES), 1)
        tri = (col <= row).astype(BF16)
        lane1 = lax.broadcasted_iota(jnp.int32, (1, LANES), 1)
        n_aug = ATT_HEADS * AUG_PER_HEAD
        k_ones = ((lane1 % AUG_PER_HEAD >= 3) & (lane1 < n_aug)).astype(F32)
        q_ones = ((lane1 % AUG_PER_HEAD < 3) & (lane1 < n_aug)).astype(F32)
        sel = [((col == AUG_PER_HEAD * row + r) & (row < ATT_HEADS)).astype(BF16)
               for r in range(AUG_PER_HEAD)]
        n_blk = seq // LANES
        local = []
        for blk in range(n_blk):
            pieces = _split3(lf_ref[blk * LANES:(blk + 1) * LANES, :] * LOG2E)
            acc = jnp.dot(tri, pieces[0], preferred_element_type=F32)
            for p in pieces[1:]:
                acc = acc + jnp.dot(tri, p, preferred_element_type=F32)
            local.append(acc)
        carry = jnp.zeros((1, LANES), F32)
        for blk in range(n_blk):
            rows = slice(blk * LANES, (blk + 1) * LANES)
            c = local[blk] + carry
            carry = c[LANES - 1:LANES, :]
            c3 = _split3(c)
            kacc = k_ones
            qacc = q_ones
            for r in range(3):
                kacc = kacc - jnp.dot(c3[r], sel[r], preferred_element_type=F32)
                qacc = qacc + jnp.dot(c3[r], sel[3 + r], preferred_element_type=F32)
            kaug_ref[rows, :] = kacc.astype(BF16)
            qaug_ref[rows, :] = qacc.astype(BF16)

    one_row = (lax.broadcasted_iota(jnp.int32, (VT_PAD, seq), 0) == 0).astype(BF16)
    for lp in range(ATT_PAIRS_PER_STEP):
        v_t = v_ref[:, lp * LANES:(lp + 1) * LANES].astype(F32).T.astype(BF16)
        for e in range(2):
            vt_ref[lp, e * VT_ROWS:e * VT_ROWS + HEAD_DIM, :] = v_t[e * HEAD_DIM:(e + 1) * HEAD_DIM, :]
            vt_ref[lp, e * VT_ROWS + HEAD_DIM:(e + 1) * VT_ROWS, :] = one_row
    lane_q = lax.broadcasted_iota(jnp.int32, (TQ, LANES), 1)
    key_i = lax.broadcasted_iota(jnp.int32, (TQ, TQ), 0)
    qry_i = lax.broadcasted_iota(jnp.int32, (TQ, TQ), 1)
    nt_dims = (((1,), (1,)), ((), ()))

    def scores(lp, i, e):
        q0 = i * TQ
        head = 2 * (ATT_PAIRS_PER_STEP * j + lp) + e
        cols = slice(lp * LANES, (lp + 1) * LANES)
        q_t = q_ref[q0:q0 + TQ, cols]
        qa_t = qaug_ref[q0:q0 + TQ, :]
        qm = jnp.where(lane_q // HEAD_DIM == e, q_t, jnp.zeros_like(q_t))
        in_head = (lane_q >= AUG_PER_HEAD * head) & (lane_q < AUG_PER_HEAD * (head + 1))
        qa = jnp.where(in_head, qa_t, jnp.zeros_like(qa_t))
        qcat = jnp.concatenate([qm, qa], axis=1)
        k_diag = jnp.concatenate([k_ref[q0:q0 + TQ, cols], kaug_ref[q0:q0 + TQ, :]], axis=1)
        s_d = lax.dot_general(k_diag, qcat, nt_dims, preferred_element_type=F32)
        s_d = jnp.where(key_i <= qry_i, s_d, MASK_VALUE)
        m = jnp.max(s_d, axis=0, keepdims=True)
        s_o = None
        if i:
            k_off = jnp.concatenate([k_ref[0:q0, cols], kaug_ref[0:q0, :]], axis=1)
            s_o = lax.dot_general(k_off, qcat, nt_dims, preferred_element_type=F32)
            m = jnp.maximum(m, jnp.max(s_o, axis=0, keepdims=True))
        return s_d, s_o, m

    def weighted_values(lp, i, e, s_d, s_o, m):
        q0 = i * TQ
        rows = slice(e * VT_ROWS, (e + 1) * VT_ROWS)
        p_d = jnp.exp2((s_d - m).astype(BF16))
        o_t = jnp.dot(vt_ref[lp, rows, q0:q0 + TQ], p_d, preferred_element_type=F32)
        if i:
            p_o = jnp.exp2((s_o - m).astype(BF16))
            o_t = o_t + jnp.dot(vt_ref[lp, rows, 0:q0], p_o, preferred_element_type=F32)
        return o_t[0:HEAD_DIM, :] * (1.0 / o_t[HEAD_DIM:HEAD_DIM + 1, :])

    tiles = [(lp, i) for lp in range(ATT_PAIRS_PER_STEP) for i in range(n_q)]
    pending = None
    for tile in tiles + [None]:
        cur = [scores(*tile, e) for e in range(2)] if tile is not None else None
        if pending is not None:
            (lp, pi), stats = pending
            y_t = [weighted_values(lp, pi, e, *stats[e]) for e in range(2)]
            q0 = pi * TQ
            cols = slice(lp * LANES, (lp + 1) * LANES)
            y = jnp.concatenate(y_t, axis=0).T
            o_ref[q0:q0 + TQ, cols] = (y * ga_ref[q0:q0 + TQ, cols].astype(F32)).astype(BF16)
        pending = (tile, cur)


def _attention(z, lf, bsz, seq):
    m = bsz * seq
    n_steps = ATT_HEADS // (2 * ATT_PAIRS_PER_STEP)
    col = lambda g: (lambda b, j: (b, g * n_steps + j))
    blk = (seq, ATT_PAIRS_PER_STEP * LANES)
    aug = (seq, LANES)
    return pl.pallas_call(
        functools.partial(_attn_kernel, seq=seq),
        grid=(bsz, n_steps),
        in_specs=[
            pl.BlockSpec(blk, col(0)),
            pl.BlockSpec(blk, col(1)),
            pl.BlockSpec(blk, col(2)),
            pl.BlockSpec(blk, col(3)),
            pl.BlockSpec(aug, lambda b, j: (b, 0)),
        ],
        out_specs=pl.BlockSpec(blk, lambda b, j: (b, j)),
        out_shape=jax.ShapeDtypeStruct((m, D_MODEL), BF16),
        scratch_shapes=[pltpu.VMEM(aug, BF16), pltpu.VMEM(aug, BF16),
                        pltpu.VMEM((ATT_PAIRS_PER_STEP, 2 * VT_ROWS, seq), BF16)],
        compiler_params=pltpu.CompilerParams(
            dimension_semantics=("arbitrary", "arbitrary"), vmem_limit_bytes=VMEM_LIMIT),
        name="fox_attention",
    )(z, z, z, z, lf)


def _shift_rows(x3, d, fill):
    n_tiles, _, width = x3.shape
    fill = jnp.broadcast_to(jnp.asarray(fill, x3.dtype), (1, SUBLANES, width))
    if d % SUBLANES == 0:
        dt = d // SUBLANES
        return jnp.concatenate([jnp.broadcast_to(fill, (dt, SUBLANES, width)),
                                x3[:n_tiles - dt]], axis=0)
    sub = lax.broadcasted_iota(jnp.int32, (1, SUBLANES, 1), 1)
    prev = jnp.concatenate([fill, x3[:n_tiles - 1]], axis=0)
    return jnp.where(sub >= d, pltpu.roll(x3, d, axis=1), pltpu.roll(prev, d, axis=1))


def _rglru_kernel(xr_ref, gr_ref, perm_ref, unperm_ref, cw_ref, cb_ref, wa_ref, ba_ref, wx_ref,
                  bx_ref, lam_ref, o_ref, xs_ref, h_ref):
    t = pl.program_id(1)
    n_grp = T_RNN // RG_STRIDE
    hist = SUBLANES

    @pl.when(t == 0)
    def _reset():
        xs_ref[:, 0:hist, :] = jnp.zeros((CONV_W - 1, hist, D_MODEL), F32)
        h_ref[...] = jnp.zeros_like(h_ref)

    blk = lambda v, r: v[r * n_grp:(r + 1) * n_grp]
    xp = jnp.dot(perm_ref[...], xr_ref[...], preferred_element_type=F32)
    grp = jnp.dot(perm_ref[...], gr_ref[...], preferred_element_type=F32)
    first = RG_STRIDE - CONV_W + 1
    x_prev = {}
    for r in range(first, RG_STRIDE):
        xs_ref[r - first, hist:hist + n_grp, :] = blk(xp, r)
        x_prev[r] = xs_ref[r - first, pl.ds(hist - 1, n_grp), :]
        xs_ref[r - first, 0:hist, :] = xs_ref[r - first, n_grp:n_grp + hist, :]
    src = lambda r: blk(xp, r) if r >= 0 else x_prev[r + RG_STRIDE]
    xc_blocks = []
    for r in range(RG_STRIDE):
        acc = cb_ref[...] + cw_ref[0:1, :] * src(r - CONV_W + 1)
        for tap in range(1, CONV_W):
            acc = acc + cw_ref[tap:tap + 1, :] * src(r - CONV_W + 1 + tap)
        xc_blocks.append(acc)
    xc = jnp.concatenate(xc_blocks, axis=0)

    xcb = xc.astype(BF16)
    half_c_sp = (0.5 * RG_C) * _softplus(-lam_ref[...])
    tiles = n_grp // SUBLANES
    for kb in range(D_MODEL // MXU_DIM):
        cols = slice(kb * MXU_DIM, (kb + 1) * MXU_DIM)
        xk = xcb[:, cols]
        t_r = jnp.tanh(jnp.dot(xk, wa_ref[kb], preferred_element_type=F32) + ba_ref[:, cols])
        t_i = jnp.tanh(jnp.dot(xk, wx_ref[kb], preferred_element_type=F32) + bx_ref[:, cols])
        neg_log_a = half_c_sp[:, cols] + half_c_sp[:, cols] * t_r
        a = jnp.exp(-neg_log_a)
        one_m_a2 = (1.0 + a * a) * jnp.tanh(neg_log_a)
        mult = jnp.where(one_m_a2 > 0.0, one_m_a2 * lax.rsqrt(one_m_a2), 0.0)
        u = mult * ((0.5 + 0.5 * t_i) * xc[:, cols])

        a_in = [blk(a, 0)]
        h_in = [blk(u, 0)]
        for r in range(1, RG_STRIDE):
            h_in.append(blk(a, r) * h_in[-1] + blk(u, r))
            a_in.append(blk(a, r) * a_in[-1])
        a_g = a_in[-1].reshape(tiles, SUBLANES, MXU_DIM)
        h_g = h_in[-1].reshape(tiles, SUBLANES, MXU_DIM)
        d = 1
        while d < n_grp:
            h_g = h_g + a_g * _shift_rows(h_g, d, 0.0)
            a_g = a_g * _shift_rows(a_g, d, 1.0)
            d *= 2
        h0 = h_ref[:, cols]
        h_end = h_g + a_g * h0
        h_ent = _shift_rows(h_end, 1, h0).reshape(n_grp, MXU_DIM)
        h_ref[:, cols] = h_end[tiles - 1, SUBLANES - 1:SUBLANES, :]
        y = jnp.concatenate(
            [(h_in[r] + a_in[r] * h_ent) * blk(grp, r)[:, cols] for r in range(RG_STRIDE)], axis=0)
        o_ref[:, cols] = jnp.dot(unperm_ref[...], y.astype(BF16),
                                 preferred_element_type=F32).astype(BF16)


def _rglru(z, cw, cb, wa_bd, ba, wx_bd, bx, lam, bsz, seq):
    m = bsz * seq
    n_t = seq // T_RNN
    const2 = lambda b, t: (0, 0)
    const3 = lambda b, t: (0, 0, 0)
    n_kb = D_MODEL // MXU_DIM
    n_grp = T_RNN // RG_STRIDE
    pos = jnp.arange(T_RNN)
    step_of_row = (pos % n_grp) * RG_STRIDE + pos // n_grp
    perm = (step_of_row[:, None] == pos[None, :]).astype(BF16)
    return pl.pallas_call(
        _rglru_kernel,
        grid=(bsz, n_t),
        in_specs=[
            pl.BlockSpec((T_RNN, D_MODEL), lambda b, t: (b * n_t + t, 4)),
            pl.BlockSpec((T_RNN, D_MODEL), lambda b, t: (b * n_t + t, 5)),
            pl.BlockSpec((T_RNN, T_RNN), const2),
            pl.BlockSpec((T_RNN, T_RNN), const2),
            pl.BlockSpec((CONV_W, D_MODEL), const2),
            pl.BlockSpec((1, D_MODEL), const2),
            pl.BlockSpec((n_kb, MXU_DIM, MXU_DIM), const3),
            pl.BlockSpec((1, D_MODEL), const2),
            pl.BlockSpec((n_kb, MXU_DIM, MXU_DIM), const3),
            pl.BlockSpec((1, D_MODEL), const2),
            pl.BlockSpec((1, D_MODEL), const2),
        ],
        out_specs=pl.BlockSpec((T_RNN, D_MODEL), lambda b, t: (b * n_t + t, 0)),
        out_shape=jax.ShapeDtypeStruct((m, D_MODEL), BF16),
        scratch_shapes=[
            pltpu.VMEM((CONV_W - 1, SUBLANES + n_grp, D_MODEL), F32),
            pltpu.VMEM((1, D_MODEL), F32),
        ],
        compiler_params=pltpu.CompilerParams(
            dimension_semantics=("arbitrary", "arbitrary"), vmem_limit_bytes=VMEM_LIMIT),
        name="rglru",
    )(z, z, perm, perm.T, cw, cb, wa_bd, ba, wx_bd, bx, lam)


def _out_kernel(x_ref, ya_ref, yr_ref, sa_ref, sr_ref, wa_ref, wr_ref, wo_ref, pw_ref, o_ref):
    sub_rows = BM_OUT // OUT_SUBTILES
    for part in range(OUT_SUBTILES):
        rows = slice(part * sub_rows, (part + 1) * sub_rows)
        pa = jnp.dot(ya_ref[rows, :], wa_ref[...], preferred_element_type=F32)
        pr = jnp.dot(yr_ref[rows, :], wr_ref[...], preferred_element_type=F32)
        mrg = sa_ref[rows, :].astype(F32) * pa + sr_ref[rows, :].astype(F32) * pr
        o = jnp.dot(mrg.astype(BF16), wo_ref[...], preferred_element_type=F32)
        var = jnp.mean(o * o, axis=-1, keepdims=True)
        o_ref[rows, :] = x_ref[rows, :] + o * lax.rsqrt(var + NORM_EPS) * pw_ref[...]


def _out_stage(x2, ya, yr, z, wa, wr, wo, pw):
    m = x2.shape[0]
    const = lambda i: (0, 0)
    row = lambda i: (i, 0)
    blk = (BM_OUT, D_MODEL)
    wblk = (D_MODEL, D_MODEL)
    return pl.pallas_call(
        _out_kernel,
        grid=(m // BM_OUT,),
        in_specs=[
            pl.BlockSpec(blk, row),
            pl.BlockSpec(blk, row),
            pl.BlockSpec(blk, row),
            pl.BlockSpec(blk, lambda i: (i, 6)),
            pl.BlockSpec(blk, lambda i: (i, 7)),
            pl.BlockSpec(wblk, const),
            pl.BlockSpec(wblk, const),
            pl.BlockSpec(wblk, const),
            pl.BlockSpec((1, D_MODEL), const),
        ],
        out_specs=pl.BlockSpec(blk, row),
        out_shape=jax.ShapeDtypeStruct((m, D_MODEL), F32),
        compiler_params=pltpu.CompilerParams(
            dimension_semantics=("arbitrary",), vmem_limit_bytes=VMEM_LIMIT),
        name="out_stage",
    )(x2, ya, yr, z, z, wa, wr, wo, pw)


def _block_diag(w):
    per = MXU_DIM // RNN_BLOCK_W
    n_kb = RNN_BLOCKS // per
    w4 = w.reshape(n_kb, per, RNN_BLOCK_W, RNN_BLOCK_W)
    eye = jnp.eye(per, dtype=w.dtype)
    bd = jnp.einsum('kgij,gh->kgihj', w4, eye)
    return bd.reshape(n_kb, MXU_DIM, MXU_DIM).astype(BF16)


def kernel(x, pre_norm_w, w_in, b_in, conv_w, conv_b, rg_wa, rg_ba, rg_wx, rg_bx, rg_lambda,
           w_branch_a, w_branch_r, w_out, post_norm_w):
    bsz, seq, d = x.shape
    assert d == D_MODEL and seq % TQ == 0 and seq % T_RNN == 0
    assert (bsz * seq) % BM_IN == 0 and (bsz * seq) % BM_OUT == 0
    d_att = ATT_HEADS * HEAD_DIM
    f0 = 3 * d_att
    f1 = f0 + ATT_HEADS
    x2 = x.reshape(bsz * seq, d)
    row = lambda v: v.reshape(1, -1)
    for l in range(pre_norm_w.shape[0]):
        w = w_in[l]
        w8 = jnp.concatenate([w[:, :f0], w[:, f1:]], axis=1).astype(BF16)
        b8 = row(jnp.concatenate([b_in[l, :f0], b_in[l, f1:]]))
        wf = jnp.pad(w[:, f0:f1], ((0, 0), (0, LANES - ATT_HEADS))).astype(BF16)
        bf = row(jnp.pad(b_in[l, f0:f1], (0, LANES - ATT_HEADS)))
        z, lf = _in_proj(x2, row(pre_norm_w[l]), w8, b8, wf, bf)
        ya = _attention(z, lf, bsz, seq)
        yr = _rglru(z, conv_w[l], row(conv_b[l]), _block_diag(0.5 * rg_wa[l]),
                    row(0.5 * rg_ba[l]), _block_diag(0.5 * rg_wx[l]), row(0.5 * rg_bx[l]),
                    row(rg_lambda[l]), bsz, seq)
        x2 = _out_stage(x2, ya, yr, z, w_branch_a[l].astype(BF16), w_branch_r[l].astype(BF16),
                        w_out[l].astype(BF16), row(post_norm_w[l]))
    return x2.reshape(bsz, seq, d)
```
